```python
import jax
import jax.numpy as jnp
from jax import lax
import numpy as np

D_MODEL = 1024
BATCH = 8
SEQ = 8192
DEPTH = 1

ATT_HEADS = 4
ATT_HEAD_DIM = 128
ATT_WIDTH = ATT_HEADS * ATT_HEAD_DIM
IDX_HEADS = 8
IDX_DIM = 64
TOPK_MAX = 256
Q_BLOCK = 128
GLA_HEADS = 4
GLA_DK = 64
GLA_DV = 128
GLA_WIDTH = GLA_HEADS * GLA_DV
GLA_GATE_RANK = 16
GLA_TAU = 16.0
GLA_CHUNK = 64
D_MIX = ATT_WIDTH + GLA_WIDTH
D_FF = ((8 * D_MODEL + 2) // 3 + 255) // 256 * 256
N_MOD = 6
NORM_EPS = 1e-6
IN_SPLITS = (ATT_WIDTH, ATT_WIDTH, ATT_WIDTH,
             IDX_HEADS * IDX_DIM, IDX_DIM, IDX_HEADS,
             GLA_HEADS * GLA_DK, GLA_HEADS * GLA_DK,
             GLA_WIDTH, GLA_WIDTH, GLA_GATE_RANK)
D_IN = sum(IN_SPLITS)

kernel_name = 'hymba_dsa_gla_adaln_block'


def _rms_norm(x, g):
    xf = x.astype(jnp.float32)
    y = xf * lax.rsqrt(jnp.mean(xf * xf, axis=-1, keepdims=True) + NORM_EPS)
    return (y * g.astype(jnp.float32)).astype(x.dtype)


def _modulate(h, shift, scale):
    return h * (1 + scale[:, None, :]) + shift[:, None, :]


def _split_cols(p, sizes):
    offs = []
    acc = 0
    for s in sizes[:-1]:
        acc += s
        offs.append(acc)
    return jnp.split(p, offs, axis=-1)


def dsa_attention(q, k, v, q_idx, k_idx, w_idx):
    B, L, H, Dh = q.shape
    n_sel = min(TOPK_MAX, L // 4)
    n_blocks = L // Q_BLOCK
    att_scale = Dh ** -0.5
    idx_scale = (IDX_HEADS ** -0.5) * (IDX_DIM ** -0.5)
    key_pos = jnp.arange(L)
    k_idx_f = k_idx.astype(jnp.float32)

    def block(i):
        start = i * Q_BLOCK
        qb = lax.dynamic_slice_in_dim(q, start, Q_BLOCK, axis=1)
        qib = lax.dynamic_slice_in_dim(q_idx, start, Q_BLOCK, axis=1)
        wb = lax.dynamic_slice_in_dim(w_idx, start, Q_BLOCK, axis=1)
        q_pos = start + jnp.arange(Q_BLOCK)
        causal = key_pos[None, :] <= q_pos[:, None]
        s_idx = jnp.einsum('bthd,bsd->bths', qib.astype(jnp.float32), k_idx_f)
        score = jnp.einsum('bths,bth->bts', jax.nn.relu(s_idx),
                           wb.astype(jnp.float32) * idx_scale)
        score = jnp.where(causal[None], score, -jnp.inf)
        _, sel = lax.top_k(score, n_sel)
        valid = sel <= q_pos[None, :, None]
        kg = jax.vmap(lambda kk, ii: kk[ii])(k, sel)
        vg = jax.vmap(lambda vv, ii: vv[ii])(v, sel)
        logits = jnp.einsum('bthd,btjhd->bthj', qb, kg).astype(jnp.float32) * att_scale
        logits = jnp.where(valid[:, :, None, :], logits, -jnp.inf)
        p = jax.nn.softmax(logits, axis=-1).astype(v.dtype)
        return jnp.einsum('bthj,btjhd->bthd', p, vg)

    out = lax.map(block, jnp.arange(n_blocks))
    return out.transpose(1, 0, 2, 3, 4).reshape(B, L, H * Dh)


def gla_chunked(q, k, v, log_a):
    B, L, H, DK = q.shape
    DV = v.shape[-1]
    C = GLA_CHUNK
    NC = L // C

    def to_chunks(t):
        return t.astype(jnp.float32).reshape(B, NC, C, H, t.shape[-1]).transpose(1, 0, 3, 2, 4)

    qc_all = to_chunks(q) * (DK ** -0.5)
    kc_all, vc_all, gc_all = to_chunks(k), to_chunks(v), to_chunks(log_a)
    tril = jnp.tril(jnp.ones((C, C), dtype=bool))

    def step(S, inp):
        qc, kc, vc, gc = inp
        b = jnp.cumsum(gc, axis=2)
        diff = b[:, :, :, None, :] - b[:, :, None, :, :]
        decay = jnp.exp(jnp.where(tril[None, None, :, :, None], diff, -jnp.inf))
        A = jnp.einsum('bhid,bhjd,bhijd->bhij', qc, kc, decay)
        o = (jnp.einsum('bhij,bhjv->bhiv', A, vc)
             + jnp.einsum('bhid,bhdv->bhiv', qc * jnp.exp(b), S))
        b_last = b[:, :, -1:, :]
        S = (jnp.exp(b_last)[:, :, 0, :, None] * S
             + jnp.einsum('bhjd,bhjv->bhdv', kc * jnp.exp(b_last - b), vc))
        return S, o

    S0 = jnp.zeros((B, H, DK, DV), jnp.float32)
    _, o = lax.scan(step, S0, (qc_all, kc_all, vc_all, gc_all))
    return o.transpose(1, 0, 3, 2, 4).reshape(B, L, H, DV).astype(v.dtype)


def setup_inputs(seed: int = 0) -> dict:
    key = jax.random.key(seed)
    ks = jax.random.split(key, 16)

    def nrm(k, shape, scale):
        return jax.random.normal(k, shape, jnp.float32) * scale

    return {
        'x': nrm(ks[0], (BATCH, SEQ, D_MODEL), 1.0),
        'c': nrm(ks[1], (BATCH, D_MODEL), 1.0),
        'w_mod': nrm(ks[2], (DEPTH, D_MODEL, N_MOD * D_MODEL), D_MODEL ** -0.5),
        'b_mod': nrm(ks[3], (DEPTH, N_MOD * D_MODEL), 0.02),
        'norm1_g': 1.0 + nrm(ks[4], (DEPTH, D_MODEL), 0.02),
        'w_in': nrm(ks[5], (DEPTH, D_MODEL, D_IN), D_MODEL ** -0.5),
        'w_gate2': nrm(ks[6], (DEPTH, GLA_GATE_RANK, GLA_HEADS * GLA_DK), GLA_GATE_RANK ** -0.5),
        'b_gate2': nrm(ks[7], (DEPTH, GLA_HEADS * GLA_DK), 0.02),
        'att_out_g': 1.0 + nrm(ks[8], (DEPTH, ATT_WIDTH), 0.02),
        'gla_out_g': 1.0 + nrm(ks[9], (DEPTH, GLA_DV), 0.02),
        'w_out': nrm(ks[10], (DEPTH, D_MIX, D_MODEL), D_MIX ** -0.5),
        'norm2_g': 1.0 + nrm(ks[11], (DEPTH, D_MODEL), 0.02),
        'w_gate_up': nrm(ks[12], (DEPTH, D_MODEL, 2 * D_FF), D_MODEL ** -0.5),
        'w_down': nrm(ks[13], (DEPTH, D_FF, D_MODEL), D_FF ** -0.5),
        'final_g': 1.0 + nrm(ks[14], (D_MODEL,), 0.02),
    }


def reference(x, c, w_mod, b_mod, norm1_g, w_in, w_gate2, b_gate2, att_out_g,
              gla_out_g, w_out, norm2_g, w_gate_up, w_down, final_g):
    B, L, _ = x.shape
    for l in range(DEPTH):
        mod = jax.nn.silu(c) @ w_mod[l] + b_mod[l]
        sh1, sc1, g1, sh2, sc2, g2 = jnp.split(mod, N_MOD, axis=-1)

        h = _modulate(_rms_norm(x, norm1_g[l]), sh1, sc1)
        proj = h @ w_in[l]
        (q, k, v, q_idx, k_idx, w_idx,
         gq, gk, gv, go, g_lr) = _split_cols(proj, IN_SPLITS)

        att = dsa_attention(q.reshape(B, L, ATT_HEADS, ATT_HEAD_DIM),
                            k.reshape(B, L, ATT_HEADS, ATT_HEAD_DIM),
                            v.reshape(B, L, ATT_HEADS, ATT_HEAD_DIM),
                            q_idx.reshape(B, L, IDX_HEADS, IDX_DIM), k_idx, w_idx)
        att = _rms_norm(att, att_out_g[l])

        log_a = jax.nn.log_sigmoid((g_lr @ w_gate2[l] + b_gate2[l]).astype(jnp.float32)) / GLA_TAU
        gla = gla_chunked(gq.reshape(B, L, GLA_HEADS, GLA_DK),
                          gk.reshape(B, L, GLA_HEADS, GLA_DK),
                          gv.reshape(B, L, GLA_HEADS, GLA_DV),
                          log_a.reshape(B, L, GLA_HEADS, GLA_DK))
        gla = _rms_norm(gla, gla_out_g[l]).reshape(B, L, GLA_WIDTH) * jax.nn.silu(go)

        mix = jnp.concatenate([att, gla], axis=-1) @ w_out[l]
        x = x + g1[:, None, :] * mix

        h2 = _modulate(_rms_norm(x, norm2_g[l]), sh2, sc2)
        gate, up = jnp.split(h2 @ w_gate_up[l], 2, axis=-1)
        x = x + g2[:, None, :] * ((jax.nn.silu(gate) * up) @ w_down[l])
    return _rms_norm(x, final_g)
```

```python
import functools

import jax
import jax.numpy as jnp
from jax import lax
from jax.experimental import pallas as pl
from jax.experimental.pallas import tpu as pltpu

F32 = jnp.float32
BF16 = jnp.bfloat16
I32 = jnp.int32
HIGHEST = lax.Precision.HIGHEST

NORM_EPS = 1e-6
ATT_HEADS = 4
ATT_HEAD_DIM = 128
ATT_WIDTH = ATT_HEADS * ATT_HEAD_DIM
IDX_HEADS = 8
IDX_DIM = 64
TOPK_MAX = 256
GLA_HEADS = 4
GLA_DK = 64
GLA_DV = 128
GLA_KW = GLA_HEADS * GLA_DK
GLA_WIDTH = GLA_HEADS * GLA_DV
GLA_GATE_RANK = 16
GLA_TAU = 16.0
N_MOD = 6

LANES = 128
MISC_W = LANES
W_IDX_OFF = IDX_DIM
G_LR_OFF = IDX_DIM + IDX_HEADS

INT_MIN = -2 ** 31
MASKED_LOGIT = -1e30
VMEM_LIMIT = 56 * 1024 * 1024

ROWS_PROJ = 512
ROWS_FFN = 256
Q_TILE = 128
K_CHUNK = 512
GLA_SUB = 16
GLA_GROUP = 128
GLA_ROWS = 512


def _dot(a, b):
    return jnp.dot(a, b, preferred_element_type=F32)


def _dot_nt(a, b):
    return lax.dot_general(a, b, (((1,), (1,)), ((), ())), preferred_element_type=F32)


def _silu(x):
    return x * jax.nn.sigmoid(x)


def _mod_body(c_ref, w_ref, b_ref, o_ref):
    s = _silu(c_ref[...])
    o_ref[...] = jnp.dot(s, w_ref[...], precision=HIGHEST,
                         preferred_element_type=F32) + b_ref[...]


def _modulation(c, w_mod, b_mod):
    B, D = c.shape
    N = w_mod.shape[1]
    tn = D
    return pl.pallas_call(
        _mod_body,
        grid=(N // tn,),
        in_specs=[pl.BlockSpec((B, D), lambda j: (0, 0)),
                  pl.BlockSpec((D, tn), lambda j: (0, j)),
                  pl.BlockSpec((1, tn), lambda j: (0, j))],
        out_specs=pl.BlockSpec((B, tn), lambda j: (0, j)),
        out_shape=jax.ShapeDtypeStruct((B, N), F32),
    )(c, w_mod, b_mod)


def _inproj_body(x_ref, sc_ref, sh_ref, g_ref, wa_ref, wm_ref, wg_ref, wvt_ref,
                 q_o, k_o, v_o, qi_o, misc_o, ki_o, gq_o, gk_o, gv_o, go_o, gvt_o):
    x = x_ref[0]
    ms = jnp.mean(x * x, axis=-1, keepdims=True)
    y = (x * lax.rsqrt(ms + NORM_EPS)) * g_ref[...]
    h = y * (1.0 + sc_ref[0]) + sh_ref[0]
    hb = h.astype(BF16)
    pa = _dot(hb, wa_ref[...])
    w = ATT_WIDTH
    q_o[0] = (pa[:, 0:w] * (ATT_HEAD_DIM ** -0.5)).astype(BF16)
    k_o[0] = pa[:, w:2 * w].astype(BF16)
    v_o[0] = pa[:, 2 * w:3 * w].astype(BF16)
    qi_o[0] = pa[:, 3 * w:4 * w].astype(BF16)
    pm = _dot(hb, wm_ref[...])
    misc_o[0] = pm
    ki_o[0] = pm[:, 0:IDX_DIM].astype(BF16)
    pg = _dot(hb, wg_ref[...])
    gq_o[0] = pg[:, 0:GLA_KW]
    gk_o[0] = pg[:, GLA_KW:2 * GLA_KW]
    gv_o[0] = pg[:, 2 * GLA_KW:2 * GLA_KW + GLA_WIDTH]
    go_o[0] = pg[:, 2 * GLA_KW + GLA_WIDTH:]
    gvt_o[0] = _dot_nt(wvt_ref[...], hb)


def _input_projection(x, sc1, sh1, norm1_g, w_in):
    B, L, D = x.shape
    tm = min(ROWS_PROJ, L)
    o = [0]
    for s in (ATT_WIDTH, ATT_WIDTH, ATT_WIDTH, IDX_HEADS * IDX_DIM, IDX_DIM, IDX_HEADS,
              GLA_KW, GLA_KW, GLA_WIDTH, GLA_WIDTH, GLA_GATE_RANK):
        o.append(o[-1] + s)
    col = lambda i: w_in[:, o[i]:o[i + 1]]
    wa = jnp.concatenate([col(0), col(1), col(2), col(3)], axis=1).astype(BF16)
    pad = jnp.zeros((D, MISC_W - IDX_DIM - IDX_HEADS - GLA_GATE_RANK), w_in.dtype)
    wm = jnp.concatenate([col(4), col(5), col(10), pad], axis=1).astype(BF16)
    wg = jnp.concatenate([col(6), col(7), col(8), col(9)], axis=1).astype(BF16)
    wvt = col(8).T.astype(BF16)

    row = lambda w, dt: jax.ShapeDtypeStruct((B, L, w), dt)
    row_spec = lambda w: pl.BlockSpec((1, tm, w), lambda b, i: (b, i, 0))
    full = lambda a: pl.BlockSpec(a.shape, lambda b, i: (0,) * a.ndim,
                                  pipeline_mode=pl.Buffered(1))
    vec = pl.BlockSpec((1, 1, D), lambda b, i: (b, 0, 0))
    return pl.pallas_call(
        _inproj_body,
        grid=(B, L // tm),
        in_specs=[row_spec(D), vec, vec, full(norm1_g), full(wa), full(wm), full(wg), full(wvt)],
        out_specs=[row_spec(ATT_WIDTH), row_spec(ATT_WIDTH), row_spec(ATT_WIDTH),
                   row_spec(IDX_HEADS * IDX_DIM), row_spec(MISC_W), row_spec(IDX_DIM),
                   row_spec(GLA_KW), row_spec(GLA_KW), row_spec(GLA_WIDTH), row_spec(GLA_WIDTH),
                   pl.BlockSpec((1, GLA_WIDTH, tm), lambda b, i: (b, 0, i))],
        out_shape=[row(ATT_WIDTH, BF16), row(ATT_WIDTH, BF16), row(ATT_WIDTH, BF16),
                   row(IDX_HEADS * IDX_DIM, BF16), row(MISC_W, F32), row(IDX_DIM, BF16),
                   row(GLA_KW, F32), row(GLA_KW, F32), row(GLA_WIDTH, F32), row(GLA_WIDTH, F32),
                   jax.ShapeDtypeStruct((B, GLA_WIDTH, L), F32)],
        compiler_params=pltpu.CompilerParams(
            dimension_semantics=("parallel", "parallel"), vmem_limit_bytes=VMEM_LIMIT),
    )(x, sc1, sh1, norm1_g, wa, wm, wg, wvt)


def _dsa_body(qi_ref, misc_ref, ki_ref, q_ref, k_ref, v_ref, o_ref,
              keys_ref, acc_ref, m_ref, l_ref, *, n_sel, seq_len):
    i = pl.program_id(1)
    tq, ck = Q_TILE, K_CHUNK
    n_chunks = ((i + 1) * tq + ck - 1) // ck
    idx_scale = (IDX_HEADS ** -0.5) * (IDX_DIM ** -0.5)

    t_pos = i * tq + lax.broadcasted_iota(I32, (tq, ck), 0)
    s_iota = lax.broadcasted_iota(I32, (tq, ck), 1)

    qi = qi_ref[0]
    w = misc_ref[0][:, W_IDX_OFF:W_IDX_OFF + IDX_HEADS] * idx_scale
    q_heads = [qi[:, h * IDX_DIM:(h + 1) * IDX_DIM] for h in range(IDX_HEADS)]
    w_cols = [w[:, h:h + 1] for h in range(IDX_HEADS)]

    def score_chunk(c, carry):
        start = pl.multiple_of(c * ck, ck)
        kc = ki_ref[0, pl.ds(start, ck), :]
        score = jnp.zeros((tq, ck), F32)
        for h in range(IDX_HEADS):
            score = score + jnp.maximum(_dot_nt(q_heads[h], kc), 0.0) * w_cols[h]
        bits = lax.bitcast_convert_type(score, I32)
        key = jnp.where(bits < 0, INT_MIN - bits, bits)
        key = jnp.where(start + s_iota <= t_pos, key, INT_MIN)
        keys_ref[c] = key
        return carry

    lax.fori_loop(0, n_chunks, score_chunk, 0)

    def count_ge(thr):
        def body(c, acc):
            kk = keys_ref[c]
            for j in range(ck // LANES):
                acc = acc + jnp.where(kk[:, j * LANES:(j + 1) * LANES] >= thr, 1.0, 0.0)
            return acc
        acc = lax.fori_loop(0, n_chunks, body, jnp.zeros((tq, LANES), F32))
        return jnp.sum(acc, axis=1, keepdims=True)

    need = float(n_sel)

    def search_step(it, p):
        cand = p + lax.shift_left(jnp.int32(1), 31 - it)
        return jnp.where(count_ge(cand) >= need, cand, p)

    kth = lax.fori_loop(0, 32, search_step, jnp.full((tq, 1), INT_MIN, I32))
    kth = jnp.maximum(kth, INT_MIN + 1)

    n_ge = count_ge(kth)
    has_tie = n_ge > need

    @pl.when(jnp.max(jnp.where(has_tie, 1.0, 0.0)) > 0.0)
    def _():
        def count_tie_before(bound):
            def body(c, acc):
                kk = keys_ref[c]
                s_pos = c * ck + s_iota
                hit = jnp.where((kk == kth) & (s_pos < bound), 1.0, 0.0)
                for j in range(ck // LANES):
                    acc = acc + hit[:, j * LANES:(j + 1) * LANES]
                return acc
            acc = lax.fori_loop(0, n_chunks, body, jnp.zeros((tq, LANES), F32))
            return jnp.sum(acc, axis=1, keepdims=True)

        def count_gt():
            def body(c, acc):
                kk = keys_ref[c]
                for j in range(ck // LANES):
                    acc = acc + jnp.where(kk[:, j * LANES:(j + 1) * LANES] > kth, 1.0, 0.0)
                return acc
            acc = lax.fori_loop(0, n_chunks, body, jnp.zeros((tq, LANES), F32))
            return jnp.sum(acc, axis=1, keepdims=True)

        keep = need - count_gt()
        n_bits = (seq_len - 1).bit_length()

        def idx_step(it, lo):
            cand = lo + lax.shift_left(jnp.int32(1), n_bits - 1 - it)
            return jnp.where(count_tie_before(cand) < keep, cand, lo)

        last = lax.fori_loop(0, n_bits, idx_step, jnp.zeros((tq, 1), I32))
        last_kept = jnp.where(has_tie, last, seq_len)

        def demote(c, carry):
            kk = keys_ref[c]
            s_pos = c * ck + s_iota
            drop = (kk == kth) & (s_pos > last_kept)
            keys_ref[c] = jnp.where(drop, kk - 1, kk)
            return carry

        lax.fori_loop(0, n_chunks, demote, 0)

    acc_ref[...] = jnp.zeros_like(acc_ref)
    m_ref[...] = jnp.full_like(m_ref, MASKED_LOGIT)
    l_ref[...] = jnp.zeros_like(l_ref)
    q = q_ref[0]

    def attend_chunk(c, carry):
        start = pl.multiple_of(c * ck, ck)
        bias = jnp.where(keys_ref[c] >= kth, 0.0, MASKED_LOGIT)
        kc = k_ref[0, pl.ds(start, ck), :]
        vc = v_ref[0, pl.ds(start, ck), :]
        for h in range(ATT_HEADS):
            sl = slice(h * ATT_HEAD_DIM, (h + 1) * ATT_HEAD_DIM)
            logits = _dot_nt(q[:, sl], kc[:, sl]) + bias
            m_old = m_ref[h]
            m_new = jnp.maximum(m_old, jnp.max(logits, axis=1, keepdims=True))
            p = jnp.exp(logits - m_new)
            alpha = jnp.exp(m_old - m_new)
            l_ref[h] = alpha * l_ref[h] + jnp.sum(p, axis=1, keepdims=True)
            acc_ref[h] = alpha * acc_ref[h] + _dot(p.astype(BF16), vc[:, sl])
            m_ref[h] = m_new
        return carry

    lax.fori_loop(0, n_chunks, attend_chunk, 0)
    for h in range(ATT_HEADS):
        o_ref[0, :, h * ATT_HEAD_DIM:(h + 1) * ATT_HEAD_DIM] = acc_ref[h] / l_ref[h]


def _dsa_attention(q, k, v, qi, ki, misc):
    B, L, _ = q.shape
    tq = Q_TILE
    n_sel = min(TOPK_MAX, L // 4)
    body = functools.partial(_dsa_body, n_sel=n_sel, seq_len=L)
    q_spec = lambda w: pl.BlockSpec((1, tq, w), lambda b, i: (b, i, 0))
    seq_spec = lambda w: pl.BlockSpec((1, L, w), lambda b, i: (b, 0, 0))
    return pl.pallas_call(
        body,
        grid=(B, L // tq),
        in_specs=[q_spec(IDX_HEADS * IDX_DIM), q_spec(MISC_W), seq_spec(IDX_DIM),
                  q_spec(ATT_WIDTH), seq_spec(ATT_WIDTH), seq_spec(ATT_WIDTH)],
        out_specs=q_spec(ATT_WIDTH),
        out_shape=jax.ShapeDtypeStruct((B, L, ATT_WIDTH), F32),
        scratch_shapes=[pltpu.VMEM((L // K_CHUNK, tq, K_CHUNK), I32),
                        pltpu.VMEM((ATT_HEADS, tq, ATT_HEAD_DIM), F32),
                        pltpu.VMEM((ATT_HEADS, tq, 1), F32),
                        pltpu.VMEM((ATT_HEADS, tq, 1), F32)],
        compiler_params=pltpu.CompilerParams(
            dimension_semantics=("parallel", "arbitrary"), vmem_limit_bytes=VMEM_LIMIT),
    )(qi, misc, ki, q, k, v)


def _gla_body(gq_ref, gk_ref, gvt_ref, gv_ref, misc_ref, w2_ref, b2_ref, o_ref, st_ref):
    n, kw, vw = GLA_SUB, GLA_KW, GLA_WIDTH

    @pl.when(pl.program_id(1) == 0)
    def _():
        st_ref[...] = jnp.zeros_like(st_ref)

    rows = gq_ref.shape[1]
    grp = GLA_GROUP
    hv = lax.broadcasted_iota(I32, (vw, kw), 0) // GLA_DV
    hd = lax.broadcasted_iota(I32, (vw, kw), 1) // GLA_DK
    st_mask = hv == hd
    ed = lax.broadcasted_iota(I32, (kw, vw), 0) // GLA_DK
    ev = lax.broadcasted_iota(I32, (kw, vw), 1) // GLA_DV
    expand = jnp.where(ed == ev, 1.0, 0.0).astype(BF16)
    row_i = lax.broadcasted_iota(I32, (n, kw), 0)
    in_blk = lax.broadcasted_iota(I32, (grp, kw), 0) % n
    scale = GLA_DK ** -0.5

    def group(gi, carry):
        g0 = pl.multiple_of(gi * grp, grp)
        g_lr = misc_ref[0, pl.ds(g0, grp), :][:, G_LR_OFF:G_LR_OFF + GLA_GATE_RANK]
        z = jnp.dot(g_lr, w2_ref[...], precision=HIGHEST,
                    preferred_element_type=F32) + b2_ref[...]
        log_a = (jnp.minimum(z, 0.0) - jnp.log1p(jnp.exp(-jnp.abs(z)))) * (1.0 / GLA_TAU)
        b_all = log_a
        shift = 1
        while shift < n:
            b_all = b_all + jnp.where(in_blk >= shift, pltpu.roll(b_all, shift, axis=0), 0.0)
            shift *= 2
        q_all = gq_ref[0, pl.ds(g0, grp), :] * scale
        k_all = gk_ref[0, pl.ds(g0, grp), :]
        v_all = gv_ref[0, pl.ds(g0, grp), :]
        vt_all = gvt_ref[0, :, pl.ds(g0, grp)]
        for blk in range(grp // n):
            sl = slice(blk * n, (blk + 1) * n)
            _gla_block(q_all[sl], k_all[sl], v_all[sl], vt_all[:, sl], b_all[sl],
                       o_ref.at[0, pl.ds(g0 + blk * n, n), :], st_ref,
                       st_mask, expand, row_i)
        return carry

    lax.fori_loop(0, rows // grp, group, 0)


def _gla_block(qb, kb, vb, vtb, b, o_blk, st_ref, st_mask, expand, row_i):
    n, vw = GLA_SUB, GLA_WIDTH
    st = st_ref[...]
    o_inter = _dot_nt((qb * jnp.exp(b)).astype(BF16), st.astype(BF16))
    terms = []
    for j in range(n):
        diff = jnp.where(row_i >= j, b - b[j:j + 1, :], -jnp.inf)
        terms.append(qb * kb[j:j + 1, :] * jnp.exp(diff))
    t_all = jnp.concatenate(terms, axis=0).astype(BF16)
    g_all = _dot(t_all, expand)
    o_intra = jnp.zeros((n, vw), F32)
    for j in range(n):
        o_intra = o_intra + g_all[j * n:(j + 1) * n, :] * vb[j:j + 1, :]
    o_blk[...] = o_inter + o_intra
    b_last = b[n - 1:n, :]
    k_dec = (kb * jnp.exp(b_last - b)).astype(BF16)
    upd = _dot(vtb.astype(BF16), k_dec)
    st_ref[...] = st * jnp.exp(b_last) + jnp.where(st_mask, upd, 0.0)


def _gla(gq, gk, gvt, gv, misc, w_gate2, b_gate2):
    B, L, _ = gq.shape
    rows = min(GLA_ROWS, L)
    row_spec = lambda w: pl.BlockSpec((1, rows, w), lambda b, i: (b, i, 0))
    full = lambda a: pl.BlockSpec(a.shape, lambda b, i: (0,) * a.ndim)
    return pl.pallas_call(
        _gla_body,
        grid=(B, L // rows),
        in_specs=[row_spec(GLA_KW), row_spec(GLA_KW),
                  pl.BlockSpec((1, GLA_WIDTH, rows), lambda b, i: (b, 0, i)),
                  row_spec(GLA_WIDTH), row_spec(MISC_W), full(w_gate2), full(b_gate2)],
        out_specs=row_spec(GLA_WIDTH),
        out_shape=jax.ShapeDtypeStruct((B, L, GLA_WIDTH), F32),
        scratch_shapes=[pltpu.VMEM((GLA_WIDTH, GLA_KW), F32)],
        compiler_params=pltpu.CompilerParams(
            dimension_semantics=("parallel", "arbitrary"), vmem_limit_bytes=VMEM_LIMIT),
    )(gq, gk, gvt, gv, misc, w_gate2, b_gate2)


def _rms(x, g):
    ms = jnp.mean(x * x, axis=-1, keepdims=True)
    return (x * lax.rsqrt(ms + NORM_EPS)) * g


def _outproj_body(x_ref, att_ref, gla_ref, go_ref, g1_ref, ag_ref, gg_ref, wo_ref, o_ref):
    a = _rms(att_ref[0], ag_ref[...])
    gl = gla_ref[0]
    parts = []
    for h in range(GLA_HEADS):
        sl = slice(h * GLA_DV, (h + 1) * GLA_DV)
        parts.append(_rms(gl[:, sl], gg_ref[...]))
    g = jnp.concatenate(parts, axis=1) * _silu(go_ref[0])
    mix = (_dot(a.astype(BF16), wo_ref[0:ATT_WIDTH, :])
           + _dot(g.astype(BF16), wo_ref[ATT_WIDTH:, :]))
    o_ref[0] = x_ref[0] + g1_ref[0] * mix


def _output_projection(x, att, gla, go, g1, att_out_g, gla_out_g, w_out):
    B, L, D = x.shape
    tm = min(ROWS_PROJ, L)
    row_spec = lambda w: pl.BlockSpec((1, tm, w), lambda b, i: (b, i, 0))
    full = lambda a: pl.BlockSpec(a.shape, lambda b, i: (0,) * a.ndim)
    vec = pl.BlockSpec((1, 1, D), lambda b, i: (b, 0, 0))
    wo = w_out.astype(BF16)
    return pl.pallas_call(
        _outproj_body,
        grid=(B, L // tm),
        in_specs=[row_spec(D), row_spec(ATT_WIDTH), row_spec(GLA_WIDTH), row_spec(GLA_WIDTH),
                  vec, full(att_out_g), full(gla_out_g), full(wo)],
        out_specs=row_spec(D),
        out_shape=jax.ShapeDtypeStruct((B, L, D), F32),
        compiler_params=pltpu.CompilerParams(
            dimension_semantics=("parallel", "parallel"), vmem_limit_bytes=VMEM_LIMIT),
    )(x, att, gla, go, g1, att_out_g, gla_out_g, wo)


def _ffn_body(x_ref, sc_ref, sh_ref, g2_ref, ng_ref, fg_ref, wgu_ref, wd_ref, o_ref, *, d_ff, final):
    x = x_ref[0]
    h = _rms(x, ng_ref[...]) * (1.0 + sc_ref[0]) + sh_ref[0]
    gu = _dot(h.astype(BF16), wgu_ref[...])
    act = _silu(gu[:, :d_ff]) * gu[:, d_ff:]
    y = x + g2_ref[0] * _dot(act.astype(BF16), wd_ref[...])
    o_ref[0] = _rms(y, fg_ref[...]) if final else y


def _ffn(x, sc2, sh2, g2, norm2_g, final_g, w_gate_up, w_down, final):
    B, L, D = x.shape
    d_ff = w_down.shape[0]
    tm = min(ROWS_FFN, L)
    row_spec = pl.BlockSpec((1, tm, D), lambda b, i: (b, i, 0))
    full = lambda a: pl.BlockSpec(a.shape, lambda b, i: (0,) * a.ndim,
                                  pipeline_mode=pl.Buffered(1))
    vec = pl.BlockSpec((1, 1, D), lambda b, i: (b, 0, 0))
    wgu = w_gate_up.astype(BF16)
    wd = w_down.astype(BF16)
    return pl.pallas_call(
        functools.partial(_ffn_body, d_ff=d_ff, final=final),
        grid=(B, L // tm),
        in_specs=[row_spec, vec, vec, vec, full(norm2_g), full(final_g), full(wgu), full(wd)],
        out_specs=row_spec,
        out_shape=jax.ShapeDtypeStruct((B, L, D), F32),
        compiler_params=pltpu.CompilerParams(
            dimension_semantics=("parallel", "parallel"), vmem_limit_bytes=VMEM_LIMIT),
    )(x, sc2, sh2, g2, norm2_g, final_g, wgu, wd)


def kernel(x, c, w_mod, b_mod, norm1_g, w_in, w_gate2, b_gate2, att_out_g, gla_out_g,
           w_out, norm2_g, w_gate_up, w_down, final_g):
    B, L, D = x.shape
    depth = w_mod.shape[0]
    for l in range(depth):
        mod = _modulation(c, w_mod[l], b_mod[l][None, :])
        sh1, sc1, g1, sh2, sc2, g2 = [m[:, None, :] for m in jnp.split(mod, N_MOD, axis=-1)]
        q, k, v, qi, misc, ki, gq, gk, gv, go, gvt = _input_projection(
            x, sc1, sh1, norm1_g[l][None, :], w_in[l])
        att = _dsa_attention(q, k, v, qi, ki, misc)
        gla = _gla(gq, gk, gvt, gv, misc, w_gate2[l], b_gate2[l][None, :])
        x = _output_projection(x, att, gla, go, g1, att_out_g[l][None, :],
                               gla_out_g[l][None, :], w_out[l])
        x = _ffn(x, sc2, sh2, g2, norm2_g[l][None, :], final_g[None, :], w_gate_up[l],
                 w_down[l], final=(l == depth - 1))
    return x
```

```python
import functools

import jax
import jax.numpy as jnp
from jax import lax
from jax.experimental import pallas as pl
from jax.experimental.pallas import tpu as pltpu

F32 = jnp.float32
BF16 = jnp.bfloat16
I32 = jnp.int32
HIGHEST = lax.Precision.HIGHEST

NORM_EPS = 1e-6
ATT_HEADS = 4
ATT_HEAD_DIM = 128
ATT_WIDTH = ATT_HEADS * ATT_HEAD_DIM
IDX_HEADS = 8
IDX_DIM = 64
TOPK_MAX = 256
GLA_HEADS = 4
GLA_DK = 64
GLA_DV = 128
GLA_KW = GLA_HEADS * GLA_DK
GLA_WIDTH = GLA_HEADS * GLA_DV
GLA_GATE_RANK = 16
GLA_TAU = 16.0
N_MOD = 6

LANES = 128
SUBLANES = 8
PACKED_ROWS = 16
MISC_W = LANES
W_IDX_OFF = IDX_DIM
G_LR_OFF = IDX_DIM + IDX_HEADS

INT_MIN = -2 ** 31
KEY_MASKED = INT_MIN
MASKED_LOGIT = -1e30
LOG2_E = 1.4426950408889634
VMEM_LIMIT = 56 * 1024 * 1024

ROWS_PROJ = 512
ROWS_FFN = 256
Q_TILE = 256
K_CHUNK = 512
SCORE_SLABS = 4
GLA_SUB = 16
GLA_GROUP = 128
GLA_ROWS = 512


def _dot(a, b):
    return jnp.dot(a, b, preferred_element_type=F32)


def _dot_nt(a, b):
    return lax.dot_general(a, b, (((1,), (1,)), ((), ())), preferred_element_type=F32)


def _silu(x):
    return x * jax.nn.sigmoid(x)


def _mod_body(c_ref, w_ref, b_ref, o_ref):
    s = _silu(c_ref[...])
    o_ref[...] = jnp.dot(s, w_ref[...], precision=HIGHEST,
                         preferred_element_type=F32) + b_ref[...]


def _modulation(c, w_mod, b_mod):
    B, D = c.shape
    N = w_mod.shape[1]
    tn = D
    return pl.pallas_call(
        _mod_body,
        grid=(N // tn,),
        in_specs=[pl.BlockSpec((B, D), lambda j: (0, 0)),
                  pl.BlockSpec((D, tn), lambda j: (0, j)),
                  pl.BlockSpec((1, tn), lambda j: (0, j))],
        out_specs=pl.BlockSpec((B, tn), lambda j: (0, j)),
        out_shape=jax.ShapeDtypeStruct((B, N), F32),
    )(c, w_mod, b_mod)


def _inproj_body(x_ref, sc_ref, sh_ref, g_ref, wa_ref, wm_ref, wg_ref, wt_ref,
                 q_o, k_o, qi_o, misc_o, ki_o, gq_o, gk_o, gv_o, go_o, vt_o, gvt_o, misct_o):
    x = x_ref[0]
    ms = jnp.mean(x * x, axis=-1, keepdims=True)
    y = (x * lax.rsqrt(ms + NORM_EPS)) * g_ref[...]
    h = y * (1.0 + sc_ref[0]) + sh_ref[0]
    hb = h.astype(BF16)
    pa = _dot(hb, wa_ref[...])
    w = ATT_WIDTH
    q_o[0] = (pa[:, 0:w] * (ATT_HEAD_DIM ** -0.5 * LOG2_E)).astype(BF16)
    k_o[0] = pa[:, w:2 * w].astype(BF16)
    qi_o[0] = pa[:, 2 * w:3 * w].astype(BF16)
    pm = _dot(hb, wm_ref[...])
    misc_o[0] = pm
    ki_o[0] = pm[:, 0:IDX_DIM].astype(BF16)
    pg = _dot(hb, wg_ref[...])
    gq_o[0] = pg[:, 0:GLA_KW]
    gk_o[0] = pg[:, GLA_KW:2 * GLA_KW]
    gv_o[0] = pg[:, 2 * GLA_KW:2 * GLA_KW + GLA_WIDTH]
    go_o[0] = pg[:, 2 * GLA_KW + GLA_WIDTH:]
    pt = _dot_nt(wt_ref[...], hb)
    vt_o[0] = pt[0:w, :].astype(BF16)
    gvt_o[0] = pt[w:w + GLA_WIDTH, :]
    misct_o[0] = pt[w + GLA_WIDTH:, :]


def _input_projection(x, sc1, sh1, norm1_g, w_in):
    B, L, D = x.shape
    tm = min(ROWS_PROJ, L)
    o = [0]
    for s in (ATT_WIDTH, ATT_WIDTH, ATT_WIDTH, IDX_HEADS * IDX_DIM, IDX_DIM, IDX_HEADS,
              GLA_KW, GLA_KW, GLA_WIDTH, GLA_WIDTH, GLA_GATE_RANK):
        o.append(o[-1] + s)
    col = lambda i: w_in[:, o[i]:o[i + 1]]
    wa = jnp.concatenate([col(0), col(1), col(3)], axis=1).astype(BF16)
    pad = jnp.zeros((D, MISC_W - IDX_DIM - IDX_HEADS - GLA_GATE_RANK), w_in.dtype)
    wm = jnp.concatenate([col(4), col(5), col(10), pad], axis=1).astype(BF16)
    wg = jnp.concatenate([col(6), col(7), col(8), col(9)], axis=1).astype(BF16)
    wt = jnp.concatenate([col(2), col(8), wm], axis=1).T.astype(BF16)

    row = lambda w, dt: jax.ShapeDtypeStruct((B, L, w), dt)
    feat = lambda w, dt: jax.ShapeDtypeStruct((B, w, L), dt)
    row_spec = lambda w: pl.BlockSpec((1, tm, w), lambda b, i: (b, i, 0))
    feat_spec = lambda w: pl.BlockSpec((1, w, tm), lambda b, i: (b, 0, i))
    full = lambda a: pl.BlockSpec(a.shape, lambda b, i: (0,) * a.ndim,
                                  pipeline_mode=pl.Buffered(1))
    vec = pl.BlockSpec((1, 1, D), lambda b, i: (b, 0, 0))
    return pl.pallas_call(
        _inproj_body,
        grid=(B, L // tm),
        in_specs=[row_spec(D), vec, vec, full(norm1_g), full(wa), full(wm), full(wg), full(wt)],
        out_specs=[row_spec(ATT_WIDTH), row_spec(ATT_WIDTH),
                   row_spec(IDX_HEADS * IDX_DIM), row_spec(MISC_W), row_spec(IDX_DIM),
                   row_spec(GLA_KW), row_spec(GLA_KW), row_spec(GLA_WIDTH), row_spec(GLA_WIDTH),
                   feat_spec(ATT_WIDTH), feat_spec(GLA_WIDTH), feat_spec(MISC_W)],
        out_shape=[row(ATT_WIDTH, BF16), row(ATT_WIDTH, BF16),
                   row(IDX_HEADS * IDX_DIM, BF16), row(MISC_W, F32), row(IDX_DIM, BF16),
                   row(GLA_KW, F32), row(GLA_KW, F32), row(GLA_WIDTH, F32), row(GLA_WIDTH, F32),
                   feat(ATT_WIDTH, BF16), feat(GLA_WIDTH, F32), feat(MISC_W, F32)],
        compiler_params=pltpu.CompilerParams(
            dimension_semantics=("parallel", "parallel"), vmem_limit_bytes=VMEM_LIMIT),
    )(x, sc1, sh1, norm1_g, wa, wm, wg, wt)


def _dsa_body(qi_ref, misct_ref, ki_ref, q_ref, k_ref, vt_ref, o_ref,
              keys_ref, d3_ref, d2_ref, d1_ref, d0_ref, qka_ref, qkb_ref, m_ref, acc_ref,
              *, n_sel, seq_len):
    i = pl.program_id(1)
    tq, ck = Q_TILE, K_CHUNK
    n_full = (i * tq) // ck
    n_chunks = n_full + 1
    idx_scale = (IDX_HEADS ** -0.5) * (IDX_DIM ** -0.5)

    s_iota = lax.broadcasted_iota(I32, (ck, tq), 0)
    t_pos = i * tq + lax.broadcasted_iota(I32, (ck, tq), 1)
    one16 = jnp.ones((), BF16)
    zero16 = jnp.zeros((), BF16)
    digit_refs = (d3_ref, d2_ref, d1_ref, d0_ref)

    qi = qi_ref[0]
    w = misct_ref[0][W_IDX_OFF:W_IDX_OFF + IDX_HEADS, :] * idx_scale
    q_heads = [qi[:, h * IDX_DIM:(h + 1) * IDX_DIM] for h in range(IDX_HEADS)]
    w_rows = [w[h:h + 1, :] for h in range(IDX_HEADS)]

    def score_chunk(c, diagonal):
        start = pl.multiple_of(c * ck, ck)
        kc = ki_ref[0, pl.ds(start, ck), :]
        slab = ck // SCORE_SLABS
        dots = [[_dot_nt(kc[r * slab:(r + 1) * slab, :], q_heads[h]) for h in range(IDX_HEADS)]
                for r in range(SCORE_SLABS)]
        slabs = []
        for r in range(SCORE_SLABS):
            part = jnp.zeros((slab, tq), F32)
            for h in range(IDX_HEADS):
                part = part + jnp.maximum(dots[r][h], 0.0) * w_rows[h]
            slabs.append(part)
        score = jnp.concatenate(slabs, axis=0)
        bits = lax.bitcast_convert_type(score, I32)
        key = jnp.where(bits < 0, INT_MIN - bits, bits)
        digits = [key >> 24, (key >> 16) & 255, (key >> 8) & 255, key & 255]
        if diagonal:
            causal = start + s_iota <= t_pos
            key = jnp.where(causal, key, KEY_MASKED)
            digits = [jnp.where(causal, digits[0], -128)] + [jnp.where(causal, d, -1)
                                                             for d in digits[1:]]
        keys_ref[c] = key
        for ref, d in zip(digit_refs, digits):
            ref[c] = d.astype(F32).astype(BF16)

    def full_chunk(c, carry):
        score_chunk(c, diagonal=False)
        return carry

    lax.fori_loop(0, n_full, full_chunk, 0)
    score_chunk(n_full, diagonal=True)

    pk = PACKED_ROWS

    def count16(ref, pred):
        def body(c, acc):
            hit = jnp.where(pred(ref[c]), one16, zero16)
            parts = [hit[j * pk:(j + 1) * pk, :] for j in range(ck // pk)]
            while len(parts) > 1:
                parts = [a + b for a, b in zip(parts[0::2], parts[1::2])]
            return acc + parts[0].astype(F32)
        acc = lax.fori_loop(0, n_chunks, body, jnp.zeros((pk, tq), F32))
        return jnp.sum(acc, axis=0, keepdims=True)

    def keep_bucket(dst_ref, sel_ref, sel_val):
        def body(c, carry):
            dst_ref[c] = jnp.where(sel_ref[c] == sel_val, dst_ref[c], -one16)
            return carry
        lax.fori_loop(0, n_chunks, body, 0)

    def digit_search(ref, rank, lowest):
        def step(it, p):
            cand = p + lax.shift_left(jnp.int32(1), 7 - it).astype(F32)
            thr = cand.astype(BF16)
            return jnp.where(count16(ref, lambda x: x >= thr) >= rank, cand, p)
        return lax.fori_loop(0, 8, step, jnp.full((1, tq), lowest, F32))

    rank = jnp.full((1, tq), float(n_sel), F32)
    kth = jnp.zeros((1, tq), I32)
    for pos, ref in enumerate(digit_refs):
        digit = digit_search(ref, rank, -128.0 if pos == 0 else 0.0)
        thr = digit.astype(BF16)
        kth = kth * 256 + digit.astype(I32)
        if pos + 1 < len(digit_refs):
            rank = rank - count16(ref, lambda x: x > thr)
            keep_bucket(digit_refs[pos + 1], ref, thr)
    need = float(n_sel)

    has_tie = count16(d0_ref, lambda x: x >= thr) > rank
    kth = jnp.maximum(kth, KEY_MASKED + 1)

    @pl.when(jnp.max(jnp.where(has_tie, 1.0, 0.0)) > 0.0)
    def _():
        def count32(hit_fn):
            def body(c, acc):
                hit = hit_fn(keys_ref[c], c * ck + s_iota)
                for j in range(ck // SUBLANES):
                    acc = acc + hit[j * SUBLANES:(j + 1) * SUBLANES, :]
                return acc
            acc = lax.fori_loop(0, n_chunks, body, jnp.zeros((SUBLANES, tq), F32))
            return jnp.sum(acc, axis=0, keepdims=True)

        def count_tie_before(bound):
            return count32(lambda kk, s_pos: jnp.where((kk == kth) & (s_pos < bound), 1.0, 0.0))

        keep = need - count32(lambda kk, s_pos: jnp.where(kk > kth, 1.0, 0.0))
        n_bits = (seq_len - 1).bit_length()

        def idx_step(it, lo):
            cand = lo + lax.shift_left(jnp.int32(1), n_bits - 1 - it)
            return jnp.where(count_tie_before(cand) < keep, cand, lo)

        last = lax.fori_loop(0, n_bits, idx_step, jnp.zeros((1, tq), I32))
        last_kept = jnp.where(has_tie, last, seq_len)

        def demote(c, carry):
            kk = keys_ref[c]
            s_pos = c * ck + s_iota
            drop = (kk == kth) & (s_pos > last_kept)
            keys_ref[c] = jnp.where(drop, kk - 1, kk)
            return carry

        lax.fori_loop(0, n_chunks, demote, 0)

    hd = ATT_HEAD_DIM
    q = q_ref[0]
    q_heads = [q[:, h * hd:(h + 1) * hd] for h in range(ATT_HEADS)]
    ones_rows = jnp.ones((pk, ck), BF16)

    heads = [slice(h * hd, (h + 1) * hd) for h in range(ATT_HEADS)]

    def qk_to(buf, c):
        kc = k_ref[0, pl.ds(pl.multiple_of(c * ck, ck), ck), :]
        for h, sl in enumerate(heads):
            buf[h] = _dot_nt(kc[:, sl], q_heads[h])

    def softmax_pv(c, buf):
        start = pl.multiple_of(c * ck, ck)
        bias = jnp.where(keys_ref[c] >= kth, 0.0, MASKED_LOGIT)
        logits = [buf[h] + bias for h in range(ATT_HEADS)]
        ms = [m_ref[h] for h in range(ATT_HEADS)]
        new_ms = [jnp.maximum(ms[h], jnp.max(logits[h], axis=0, keepdims=True))
                  for h in range(ATT_HEADS)]
        ps = [jnp.exp2(logits[h] - new_ms[h]).astype(BF16) for h in range(ATT_HEADS)]
        for h, sl in enumerate(heads):
            vt_ext = jnp.concatenate([vt_ref[0, sl, pl.ds(start, ck)], ones_rows], axis=0)
            acc_ref[h] = jnp.exp2(ms[h] - new_ms[h]) * acc_ref[h] + _dot(vt_ext, ps[h])
            m_ref[h] = new_ms[h]

    m_ref[...] = jnp.full_like(m_ref, MASKED_LOGIT)
    acc_ref[...] = jnp.zeros_like(acc_ref)
    qk_to(qka_ref, 0)

    def attend_pair(j, carry):
        c = 2 * j
        qk_to(qkb_ref, c + 1)
        softmax_pv(c, qka_ref)
        qk_to(qka_ref, c + 2)
        softmax_pv(c + 1, qkb_ref)
        return carry

    lax.fori_loop(0, n_full // 2, attend_pair, 0)

    @pl.when(n_full % 2 == 0)
    def _():
        softmax_pv(n_full, qka_ref)

    @pl.when(n_full % 2 == 1)
    def _():
        qk_to(qkb_ref, n_full)
        softmax_pv(n_full - 1, qka_ref)
        softmax_pv(n_full, qkb_ref)

    for h in range(ATT_HEADS):
        o_ref[0, h * hd:(h + 1) * hd, :] = acc_ref[h, :hd, :] / acc_ref[h, hd:hd + 1, :]


def _dsa_attention(q, k, vt, qi, ki, misct):
    B, L, _ = q.shape
    tq = min(Q_TILE, L)
    n_sel = min(TOPK_MAX, L // 4)
    body = functools.partial(_dsa_body, n_sel=n_sel, seq_len=L)
    q_spec = lambda w: pl.BlockSpec((1, tq, w), lambda b, i: (b, i, 0))
    feat_spec = lambda w: pl.BlockSpec((1, w, tq), lambda b, i: (b, 0, i))
    seq_spec = lambda a: pl.BlockSpec((1,) + a.shape[1:], lambda b, i: (b, 0, 0),
                                      pipeline_mode=pl.Buffered(1))
    digits = pltpu.VMEM((L // K_CHUNK, K_CHUNK, tq), BF16)
    return pl.pallas_call(
        body,
        grid=(B, L // tq),
        in_specs=[q_spec(IDX_HEADS * IDX_DIM), feat_spec(MISC_W), seq_spec(ki),
                  q_spec(ATT_WIDTH), seq_spec(k), seq_spec(vt)],
        out_specs=feat_spec(ATT_WIDTH),
        out_shape=jax.ShapeDtypeStruct((B, ATT_WIDTH, L), F32),
        scratch_shapes=[pltpu.VMEM((L // K_CHUNK, K_CHUNK, tq), I32)] + [digits] * 4 + [
            pltpu.VMEM((ATT_HEADS, K_CHUNK, tq), F32), pltpu.VMEM((ATT_HEADS, K_CHUNK, tq), F32),
            pltpu.VMEM((ATT_HEADS, 1, tq), F32),
            pltpu.VMEM((ATT_HEADS, ATT_HEAD_DIM + PACKED_ROWS, tq), F32)],
        compiler_params=pltpu.CompilerParams(
            dimension_semantics=("parallel", "arbitrary"), vmem_limit_bytes=VMEM_LIMIT),
    )(qi, misct, ki, q, k, vt)


def _gla_body(gq_ref, gk_ref, gvt_ref, gv_ref, misc_ref, w2_ref, b2_ref, o_ref, st_ref):
    n, kw, vw = GLA_SUB, GLA_KW, GLA_WIDTH

    @pl.when(pl.program_id(1) == 0)
    def _():
        st_ref[...] = jnp.zeros_like(st_ref)

    rows = gq_ref.shape[1]
    grp = GLA_GROUP
    hv = lax.broadcasted_iota(I32, (vw, kw), 0) // GLA_DV
    hd = lax.broadcasted_iota(I32, (vw, kw), 1) // GLA_DK
    st_mask = hv == hd
    ed = lax.broadcasted_iota(I32, (kw, vw), 0) // GLA_DK
    ev = lax.broadcasted_iota(I32, (kw, vw), 1) // GLA_DV
    expand = jnp.where(ed == ev, 1.0, 0.0).astype(BF16)
    row_i = lax.broadcasted_iota(I32, (n, kw), 0)
    in_blk = lax.broadcasted_iota(I32, (grp, kw), 0) % n
    scale = GLA_DK ** -0.5

    def group(gi, carry):
        g0 = pl.multiple_of(gi * grp, grp)
        g_lr = misc_ref[0, pl.ds(g0, grp), :][:, G_LR_OFF:G_LR_OFF + GLA_GATE_RANK]
        z = jnp.dot(g_lr, w2_ref[...], precision=HIGHEST,
                    preferred_element_type=F32) + b2_ref[...]
        log_a = (jnp.minimum(z, 0.0) - jnp.log1p(jnp.exp(-jnp.abs(z)))) * (1.0 / GLA_TAU)
        b_all = log_a
        shift = 1
        while shift < n:
            b_all = b_all + jnp.where(in_blk >= shift, pltpu.roll(b_all, shift, axis=0), 0.0)
            shift *= 2
        q_all = gq_ref[0, pl.ds(g0, grp), :] * scale
        k_all = gk_ref[0, pl.ds(g0, grp), :]
        v_all = gv_ref[0, pl.ds(g0, grp), :]
        vt_all = gvt_ref[0, :, pl.ds(g0, grp)]
        for blk in range(grp // n):
            sl = slice(blk * n, (blk + 1) * n)
            _gla_block(q_all[sl], k_all[sl], v_all[sl], vt_all[:, sl], b_all[sl],
                       o_ref.at[0, pl.ds(g0 + blk * n, n), :], st_ref,
                       st_mask, expand, row_i)
        return carry

    lax.fori_loop(0, rows // grp, group, 0)


def _gla_block(qb, kb, vb, vtb, b, o_blk, st_ref, st_mask, expand, row_i):
    n, vw = GLA_SUB, GLA_WIDTH
    st = st_ref[...]
    o_inter = _dot_nt((qb * jnp.exp(b)).astype(BF16), st.astype(BF16))
    terms = []
    for j in range(n):
        diff = jnp.where(row_i >= j, b - b[j:j + 1, :], -jnp.inf)
        terms.append(qb * kb[j:j + 1, :] * jnp.exp(diff))
    t_all = jnp.concatenate(terms, axis=0).astype(BF16)
    g_all = _dot(t_all, expand)
    o_intra = jnp.zeros((n, vw), F32)
    for j in range(n):
        o_intra = o_intra + g_all[j * n:(j + 1) * n, :] * vb[j:j + 1, :]
    o_blk[...] = o_inter + o_intra
    b_last = b[n - 1:n, :]
    k_dec = (kb * jnp.exp(b_last - b)).astype(BF16)
    upd = _dot(vtb.astype(BF16), k_dec)
    st_ref[...] = st * jnp.exp(b_last) + jnp.where(st_mask, upd, 0.0)


def _gla(gq, gk, gvt, gv, misc, w_gate2, b_gate2):
    B, L, _ = gq.shape
    rows = min(GLA_ROWS, L)
    row_spec = lambda w: pl.BlockSpec((1, rows, w), lambda b, i: (b, i, 0))
    full = lambda a: pl.BlockSpec(a.shape, lambda b, i: (0,) * a.ndim)
    return pl.pallas_call(
        _gla_body,
        grid=(B, L // rows),
        in_specs=[row_spec(GLA_KW), row_spec(GLA_KW),
                  pl.BlockSpec((1, GLA_WIDTH, rows), lambda b, i: (b, 0, i)),
                  row_spec(GLA_WIDTH), row_spec(MISC_W), full(w_gate2), full(b_gate2)],
        out_specs=row_spec(GLA_WIDTH),
        out_shape=jax.ShapeDtypeStruct((B, L, GLA_WIDTH), F32),
        scratch_shapes=[pltpu.VMEM((GLA_WIDTH, GLA_KW), F32)],
        compiler_params=pltpu.CompilerParams(
            dimension_semantics=("parallel", "arbitrary"), vmem_limit_bytes=VMEM_LIMIT),
    )(gq, gk, gvt, gv, misc, w_gate2, b_gate2)


def _rms(x, g):
    ms = jnp.mean(x * x, axis=-1, keepdims=True)
    return (x * lax.rsqrt(ms + NORM_EPS)) * g


def _outproj_body(x_ref, attt_ref, gla_ref, go_ref, g1_ref, ag_ref, gg_ref, wo_ref, o_ref):
    a = _rms(attt_ref[0].T, ag_ref[...])
    gl = gla_ref[0]
    parts = []
    for h in range(GLA_HEADS):
        sl = slice(h * GLA_DV, (h + 1) * GLA_DV)
        parts.append(_rms(gl[:, sl], gg_ref[...]))
    g = jnp.concatenate(parts, axis=1) * _silu(go_ref[0])
    mix = (_dot(a.astype(BF16), wo_ref[0:ATT_WIDTH, :])
           + _dot(g.astype(BF16), wo_ref[ATT_WIDTH:, :]))
    o_ref[0] = x_ref[0] + g1_ref[0] * mix


def _output_projection(x, att, gla, go, g1, att_out_g, gla_out_g, w_out):
    B, L, D = x.shape
    tm = min(ROWS_PROJ, L)
    row_spec = lambda w: pl.BlockSpec((1, tm, w), lambda b, i: (b, i, 0))
    full = lambda a: pl.BlockSpec(a.shape, lambda b, i: (0,) * a.ndim)
    vec = pl.BlockSpec((1, 1, D), lambda b, i: (b, 0, 0))
    wo = w_out.astype(BF16)
    return pl.pallas_call(
        _outproj_body,
        grid=(B, L // tm),
        in_specs=[row_spec(D), pl.BlockSpec((1, ATT_WIDTH, tm), lambda b, i: (b, 0, i)),
                  row_spec(GLA_WIDTH), row_spec(GLA_WIDTH),
                  vec, full(att_out_g), full(gla_out_g), full(wo)],
        out_specs=row_spec(D),
        out_shape=jax.ShapeDtypeStruct((B, L, D), F32),
        compiler_params=pltpu.CompilerParams(
            dimension_semantics=("parallel", "parallel"), vmem_limit_bytes=VMEM_LIMIT),
    )(x, att, gla, go, g1, att_out_g, gla_out_g, wo)


def _ffn_body(x_ref, sc_ref, sh_ref, g2_ref, ng_ref, fg_ref, wgu_ref, wd_ref, o_ref, *, d_ff, final):
    x = x_ref[0]
    h = _rms(x, ng_ref[...]) * (1.0 + sc_ref[0]) + sh_ref[0]
    gu = _dot(h.astype(BF16), wgu_ref[...])
    act = _silu(gu[:, :d_ff]) * gu[:, d_ff:]
    y = x + g2_ref[0] * _dot(act.astype(BF16), wd_ref[...])
    o_ref[0] = _rms(y, fg_ref[...]) if final else y


def _ffn(x, sc2, sh2, g2, norm2_g, final_g, w_gate_up, w_down, final):
    B, L, D = x.shape
    d_ff = w_down.shape[0]
    tm = min(ROWS_FFN, L)
    row_spec = pl.BlockSpec((1, tm, D), lambda b, i: (b, i, 0))
    full = lambda a: pl.BlockSpec(a.shape, lambda b, i: (0,) * a.ndim,
                                  pipeline_mode=pl.Buffered(1))
    vec = pl.BlockSpec((1, 1, D), lambda b, i: (b, 0, 0))
    wgu = w_gate_up.astype(BF16)
    wd = w_down.astype(BF16)
    return pl.pallas_call(
        functools.partial(_ffn_body, d_ff=d_ff, final=final),
        grid=(B, L // tm),
        in_specs=[row_spec, vec, vec, vec, full(norm2_g), full(final_g), full(wgu), full(wd)],
        out_specs=row_spec,
        out_shape=jax.ShapeDtypeStruct((B, L, D), F32),
        compiler_params=pltpu.CompilerParams(
            dimension_semantics=("parallel", "parallel"), vmem_limit_bytes=VMEM_LIMIT),
    )(x, sc2, sh2, g2, norm2_g, final_g, wgu, wd)


def kernel(x, c, w_mod, b_mod, norm1_g, w_in, w_gate2, b_gate2, att_out_g, gla_out_g,
           w_out, norm2_g, w_gate_up, w_down, final_g):
    B, L, D = x.shape
    depth = w_mod.shape[0]
    for l in range(depth):
        mod = _modulation(c, w_mod[l], b_mod[l][None, :])
        sh1, sc1, g1, sh2, sc2, g2 = [m[:, None, :] for m in jnp.split(mod, N_MOD, axis=-1)]
        q, k, qi, misc, ki, gq, gk, gv, go, vt, gvt, misct = _input_projection(
            x, sc1, sh1, norm1_g[l][None, :], w_in[l])
        att_t = _dsa_attention(q, k, vt, qi, ki, misct)
        gla = _gla(gq, gk, gvt, gv, misc, w_gate2[l], b_gate2[l][None, :])
        x = _output_projection(x, att_t, gla, go, g1, att_out_g[l][None, :],
                               gla_out_g[l][None, :], w_out[l])
        x = _ffn(x, sc2, sh2, g2, norm2_g[l][None, :], final_g[None, :], w_gate_up[l],
                 w_down[l], final=(l == depth - 1))
    return x
```

```python
import functools

import jax
import jax.numpy as jnp
from jax import lax
from jax.experimental import pallas as pl
from jax.experimental.pallas import tpu as pltpu

F32 = jnp.float32
BF16 = jnp.bfloat16
I32 = jnp.int32
HIGHEST = lax.Precision.HIGHEST

NORM_EPS = 1e-6
ATT_HEADS = 4
ATT_HEAD_DIM = 128
ATT_WIDTH = ATT_HEADS * ATT_HEAD_DIM
IDX_HEADS = 8
IDX_DIM = 64
TOPK_MAX = 256
GLA_HEADS = 4
GLA_DK = 64
GLA_DV = 128
GLA_KW = GLA_HEADS * GLA_DK
GLA_WIDTH = GLA_HEADS * GLA_DV
GLA_GATE_RANK = 16
GLA_TAU = 16.0
N_MOD = 6

LANES = 128
SUBLANES = 8
PACKED_ROWS = 16
MISC_W = LANES
W_IDX_OFF = IDX_DIM
G_LR_OFF = IDX_DIM + IDX_HEADS

INT_MIN = -2 ** 31
KEY_MASKED = INT_MIN
MASKED_LOGIT = -1e30
LOG2_E = 1.4426950408889634
VMEM_LIMIT = 56 * 1024 * 1024

ROWS_PROJ = 512
ROWS_FFN = 256
Q_TILE = 256
K_CHUNK = 512
WORD_BITS = 32
GROUP_KEYS = WORD_BITS * SUBLANES
GROUPS_PER_CHUNK = K_CHUNK // GROUP_KEYS
SEARCH_GROUPS = 8
GLA_SUB = 16
GLA_GROUP = 128
GLA_ROWS = 512


def _dot(a, b):
    return jnp.dot(a, b, preferred_element_type=F32)


def _dot_nt(a, b):
    return lax.dot_general(a, b, (((1,), (1,)), ((), ())), preferred_element_type=F32)


def _silu(x):
    return x * jax.nn.sigmoid(x)


def _mod_body(c_ref, w_ref, b_ref, o_ref):
    s = _silu(c_ref[...])
    o_ref[...] = jnp.dot(s, w_ref[...], precision=HIGHEST,
                         preferred_element_type=F32) + b_ref[...]


def _modulation(c, w_mod, b_mod):
    B, D = c.shape
    N = w_mod.shape[1]
    tn = D
    return pl.pallas_call(
        _mod_body,
        grid=(N // tn,),
        in_specs=[pl.BlockSpec((B, D), lambda j: (0, 0)),
                  pl.BlockSpec((D, tn), lambda j: (0, j)),
                  pl.BlockSpec((1, tn), lambda j: (0, j))],
        out_specs=pl.BlockSpec((B, tn), lambda j: (0, j)),
        out_shape=jax.ShapeDtypeStruct((B, N), F32),
    )(c, w_mod, b_mod)


def _inproj_body(x_ref, sc_ref, sh_ref, g_ref, wa_ref, wm_ref, wg_ref, wt_ref,
                 q_o, k_o, qi_o, misc_o, ki_o, gq_o, gk_o, gv_o, go_o, vt_o, gvt_o, misct_o):
    x = x_ref[0]
    ms = jnp.mean(x * x, axis=-1, keepdims=True)
    y = (x * lax.rsqrt(ms + NORM_EPS)) * g_ref[...]
    h = y * (1.0 + sc_ref[0]) + sh_ref[0]
    hb = h.astype(BF16)
    pa = _dot(hb, wa_ref[...])
    w = ATT_WIDTH
    q_o[0] = (pa[:, 0:w] * (ATT_HEAD_DIM ** -0.5 * LOG2_E)).astype(BF16)
    k_o[0] = pa[:, w:2 * w].astype(BF16)
    qi_o[0] = pa[:, 2 * w:3 * w].astype(BF16)
    pm = _dot(hb, wm_ref[...])
    misc_o[0] = pm
    ki_o[0] = pm[:, 0:IDX_DIM].astype(BF16)
    pg = _dot(hb, wg_ref[...])
    gq_o[0] = pg[:, 0:GLA_KW]
    gk_o[0] = pg[:, GLA_KW:2 * GLA_KW]
    gv_o[0] = pg[:, 2 * GLA_KW:2 * GLA_KW + GLA_WIDTH]
    go_o[0] = pg[:, 2 * GLA_KW + GLA_WIDTH:]
    pt = _dot_nt(wt_ref[...], hb)
    vt_o[0] = pt[0:w, :].astype(BF16)
    gvt_o[0] = pt[w:w + GLA_WIDTH, :]
    misct_o[0] = pt[w + GLA_WIDTH:, :]


def _input_projection(x, sc1, sh1, norm1_g, w_in):
    B, L, D = x.shape
    tm = min(ROWS_PROJ, L)
    o = [0]
    for s in (ATT_WIDTH, ATT_WIDTH, ATT_WIDTH, IDX_HEADS * IDX_DIM, IDX_DIM, IDX_HEADS,
              GLA_KW, GLA_KW, GLA_WIDTH, GLA_WIDTH, GLA_GATE_RANK):
        o.append(o[-1] + s)
    col = lambda i: w_in[:, o[i]:o[i + 1]]
    wa = jnp.concatenate([col(0), col(1), col(3)], axis=1).astype(BF16)
    pad = jnp.zeros((D, MISC_W - IDX_DIM - IDX_HEADS - GLA_GATE_RANK), w_in.dtype)
    wm = jnp.concatenate([col(4), col(5), col(10), pad], axis=1).astype(BF16)
    wg = jnp.concatenate([col(6), col(7), col(8), col(9)], axis=1).astype(BF16)
    wt = jnp.concatenate([col(2), col(8), wm], axis=1).T.astype(BF16)

    row = lambda w, dt: jax.ShapeDtypeStruct((B, L, w), dt)
    feat = lambda w, dt: jax.ShapeDtypeStruct((B, w, L), dt)
    row_spec = lambda w: pl.BlockSpec((1, tm, w), lambda b, i: (b, i, 0))
    feat_spec = lambda w: pl.BlockSpec((1, w, tm), lambda b, i: (b, 0, i))
    full = lambda a: pl.BlockSpec(a.shape, lambda b, i: (0,) * a.ndim,
                                  pipeline_mode=pl.Buffered(1))
    vec = pl.BlockSpec((1, 1, D), lambda b, i: (b, 0, 0))
    return pl.pallas_call(
        _inproj_body,
        grid=(B, L // tm),
        in_specs=[row_spec(D), vec, vec, full(norm1_g), full(wa), full(wm), full(wg), full(wt)],
        out_specs=[row_spec(ATT_WIDTH), row_spec(ATT_WIDTH),
                   row_spec(IDX_HEADS * IDX_DIM), row_spec(MISC_W), row_spec(IDX_DIM),
                   row_spec(GLA_KW), row_spec(GLA_KW), row_spec(GLA_WIDTH), row_spec(GLA_WIDTH),
                   feat_spec(ATT_WIDTH), feat_spec(GLA_WIDTH), feat_spec(MISC_W)],
        out_shape=[row(ATT_WIDTH, BF16), row(ATT_WIDTH, BF16),
                   row(IDX_HEADS * IDX_DIM, BF16), row(MISC_W, F32), row(IDX_DIM, BF16),
                   row(GLA_KW, F32), row(GLA_KW, F32), row(GLA_WIDTH, F32), row(GLA_WIDTH, F32),
                   feat(ATT_WIDTH, BF16), feat(GLA_WIDTH, F32), feat(MISC_W, F32)],
        compiler_params=pltpu.CompilerParams(
            dimension_semantics=("parallel", "parallel"), vmem_limit_bytes=VMEM_LIMIT),
    )(x, sc1, sh1, norm1_g, wa, wm, wg, wt)


def _bit_transpose(words):
    x = list(words)
    for shift, mask in ((16, 0x0000FFFF), (8, 0x00FF00FF), (4, 0x0F0F0F0F),
                        (2, 0x33333333), (1, 0x55555555)):
        for lo in range(WORD_BITS):
            if lo & shift == 0:
                a, b = x[lo], x[lo + shift]
                t = (a ^ lax.shift_right_logical(b, shift)) & mask
                x[lo] = a ^ t
                x[lo + shift] = b ^ lax.shift_left(t, shift)
    return x


def _dsa_body(qi_ref, misct_ref, ki_ref, q_ref, k_ref, vt_ref, o_ref,
              keys_ref, planes_ref, cand_ref, dta_ref, dtb_ref, qka_ref, qkb_ref, m_ref, acc_ref,
              *, n_sel, seq_len):
    i = pl.program_id(1)
    tq, ck = Q_TILE, K_CHUNK
    n_full = (i * tq) // ck
    n_chunks = n_full + 1
    idx_scale = (IDX_HEADS ** -0.5) * (IDX_DIM ** -0.5)

    s_iota = lax.broadcasted_iota(I32, (ck, tq), 0)
    t_pos = i * tq + lax.broadcasted_iota(I32, (ck, tq), 1)
    qi = qi_ref[0]
    w = misct_ref[0][W_IDX_OFF:W_IDX_OFF + IDX_HEADS, :] * idx_scale
    q_heads = [qi[:, h * IDX_DIM:(h + 1) * IDX_DIM] for h in range(IDX_HEADS)]
    w_rows = [w[h:h + 1, :] for h in range(IDX_HEADS)]

    half = GROUP_KEYS
    s_half = lax.broadcasted_iota(I32, (half, tq), 0)
    t_half = i * tq + lax.broadcasted_iota(I32, (half, tq), 1)

    def dots_to(buf, u):
        kc = ki_ref[0, pl.ds(pl.multiple_of(u * half, half), half), :]
        for h in range(IDX_HEADS):
            buf[h] = _dot_nt(kc, q_heads[h])

    def keys_from(buf, c, second, diagonal):
        u = 2 * c + second
        score = jnp.zeros((half, tq), F32)
        for h in range(IDX_HEADS):
            score = score + jnp.maximum(buf[h], 0.0) * w_rows[h]
        bits = lax.bitcast_convert_type(score, I32)
        key = jnp.where(bits < 0, INT_MIN - bits, bits)
        if diagonal:
            key = jnp.where(u * half + s_half <= t_half, key, KEY_MASKED)
        keys_ref[c, second * half:(second + 1) * half, :] = key

    dots_to(dta_ref, 0)

    def score_pair(c, carry):
        dots_to(dtb_ref, 2 * c + 1)
        keys_from(dta_ref, c, 0, diagonal=False)
        dots_to(dta_ref, 2 * c + 2)
        keys_from(dtb_ref, c, 1, diagonal=False)
        return carry

    lax.fori_loop(0, n_full, score_pair, 0)
    n_halves = GROUPS_PER_CHUNK * n_chunks

    @pl.when(i % 2 == 0)
    def _():
        keys_from(dta_ref, n_full, 0, diagonal=True)
        keys_ref[n_full, half:, :] = jnp.full((half, tq), KEY_MASKED, I32)

    @pl.when(i % 2 == 1)
    def _():
        dots_to(dtb_ref, i)
        keys_from(dta_ref, n_full, 0, diagonal=False)
        keys_from(dtb_ref, n_full, 1, diagonal=True)

    def slice_chunk(c, carry):
        for second in range(GROUPS_PER_CHUNK):
            u = GROUPS_PER_CHUNK * c + second
            for lb in range(tq // LANES):
                lanes = slice(lb * LANES, (lb + 1) * LANES)
                rows = [keys_ref[c, second * half + m * SUBLANES:
                                 second * half + (m + 1) * SUBLANES, lanes] ^ INT_MIN
                        for m in range(WORD_BITS)]
                for p, plane in enumerate(_bit_transpose(rows)):
                    planes_ref[p, u, :, lanes] = plane
            cand_ref[u] = jnp.full((SUBLANES, tq), -1, I32)
        return carry

    lax.fori_loop(0, n_chunks, slice_chunk, 0)

    sg = SEARCH_GROUPS
    n_steps = (n_halves + sg - 1) // sg

    def pad_group(g, carry):
        planes_ref[:, g] = jnp.zeros((WORD_BITS, SUBLANES, tq), I32)
        cand_ref[g] = jnp.zeros((SUBLANES, tq), I32)
        return carry

    lax.fori_loop(n_halves, sg * n_steps, pad_group, 0)

    def count_over_groups(words_of):
        def body(step, acc):
            words = words_of(pl.ds(pl.multiple_of(step * sg, sg), sg))
            return acc + jnp.sum(lax.population_count(words).astype(F32), axis=0)
        acc = lax.fori_loop(0, n_steps, body, jnp.zeros((SUBLANES, tq), F32))
        return jnp.sum(acc, axis=0, keepdims=True)

    def decide(cnt, rank, kth_u):
        take = cnt >= rank
        return (jnp.where(take, 0, -1), jnp.where(take, rank, rank - cnt),
                lax.shift_left(kth_u, 1) | jnp.where(take, 1, 0))

    need = float(n_sel)
    carry = decide(count_over_groups(lambda gs: cand_ref[gs] & planes_ref[0, gs]),
                   jnp.full((1, tq), need, F32), jnp.zeros((1, tq), I32))

    def search_pass(p, carry):
        flip, rank, kth_u = carry

        def candidates_with_bit(gs):
            cand = cand_ref[gs] & (planes_ref[p - 1, gs] ^ flip)
            cand_ref[gs] = cand
            return cand & planes_ref[p, gs]

        return decide(count_over_groups(candidates_with_bit), rank, kth_u)

    flip, rank, kth_u = lax.fori_loop(1, WORD_BITS, search_pass, carry)

    n_equal = count_over_groups(
        lambda gs: cand_ref[gs] & (planes_ref[WORD_BITS - 1, gs] ^ flip))
    kth = kth_u ^ INT_MIN
    has_tie = (n_equal > rank) & (kth > KEY_MASKED)
    kth = jnp.maximum(kth, KEY_MASKED + 1)

    @pl.when(jnp.max(jnp.where(has_tie, 1.0, 0.0)) > 0.0)
    def _():
        def count32(hit_fn):
            def body(c, acc):
                hit = hit_fn(keys_ref[c], c * ck + s_iota)
                for j in range(ck // SUBLANES):
                    acc = acc + hit[j * SUBLANES:(j + 1) * SUBLANES, :]
                return acc
            acc = lax.fori_loop(0, n_chunks, body, jnp.zeros((SUBLANES, tq), F32))
            return jnp.sum(acc, axis=0, keepdims=True)

        def count_tie_before(bound):
            return count32(lambda kk, s_pos: jnp.where((kk == kth) & (s_pos < bound), 1.0, 0.0))

        keep = need - count32(lambda kk, s_pos: jnp.where(kk > kth, 1.0, 0.0))
        n_bits = (seq_len - 1).bit_length()

        def idx_step(it, lo):
            cand = lo + lax.shift_left(jnp.int32(1), n_bits - 1 - it)
            return jnp.where(count_tie_before(cand) < keep, cand, lo)

        last = lax.fori_loop(0, n_bits, idx_step, jnp.zeros((1, tq), I32))
        last_kept = jnp.where(has_tie, last, seq_len)

        def demote(c, carry):
            kk = keys_ref[c]
            s_pos = c * ck + s_iota
            drop = (kk == kth) & (s_pos > last_kept)
            keys_ref[c] = jnp.where(drop, kk - 1, kk)
            return carry

        lax.fori_loop(0, n_chunks, demote, 0)

    hd = ATT_HEAD_DIM
    q = q_ref[0]
    q_heads = [q[:, h * hd:(h + 1) * hd] for h in range(ATT_HEADS)]
    ones_rows = jnp.ones((PACKED_ROWS, ck), BF16)

    heads = [slice(h * hd, (h + 1) * hd) for h in range(ATT_HEADS)]

    def qk_to(buf, c):
        kc = k_ref[0, pl.ds(pl.multiple_of(c * ck, ck), ck), :]
        for h, sl in enumerate(heads):
            buf[h] = _dot_nt(kc[:, sl], q_heads[h])

    def softmax_pv(c, buf):
        start = pl.multiple_of(c * ck, ck)
        bias = jnp.where(keys_ref[c] >= kth, 0.0, MASKED_LOGIT)
        logits = [buf[h] + bias for h in range(ATT_HEADS)]
        ms = [m_ref[h] for h in range(ATT_HEADS)]
        new_ms = [jnp.maximum(ms[h], jnp.max(logits[h], axis=0, keepdims=True))
                  for h in range(ATT_HEADS)]
        ps = [jnp.exp2(logits[h] - new_ms[h]).astype(BF16) for h in range(ATT_HEADS)]
        for h, sl in enumerate(heads):
            vt_ext = jnp.concatenate([vt_ref[0, sl, pl.ds(start, ck)], ones_rows], axis=0)
            acc_ref[h] = jnp.exp2(ms[h] - new_ms[h]) * acc_ref[h] + _dot(vt_ext, ps[h])
            m_ref[h] = new_ms[h]

    m_ref[...] = jnp.full_like(m_ref, MASKED_LOGIT)
    acc_ref[...] = jnp.zeros_like(acc_ref)
    qk_to(qka_ref, 0)

    def attend_pair(j, carry):
        c = 2 * j
        qk_to(qkb_ref, c + 1)
        softmax_pv(c, qka_ref)
        qk_to(qka_ref, c + 2)
        softmax_pv(c + 1, qkb_ref)
        return carry

    lax.fori_loop(0, n_full // 2, attend_pair, 0)

    @pl.when(n_full % 2 == 0)
    def _():
        softmax_pv(n_full, qka_ref)

    @pl.when(n_full % 2 == 1)
    def _():
        qk_to(qkb_ref, n_full)
        softmax_pv(n_full - 1, qka_ref)
        softmax_pv(n_full, qkb_ref)

    for h in range(ATT_HEADS):
        o_ref[0, h * hd:(h + 1) * hd, :] = acc_ref[h, :hd, :] / acc_ref[h, hd:hd + 1, :]


def _dsa_attention(q, k, vt, qi, ki, misct):
    B, L, _ = q.shape
    tq = Q_TILE
    assert tq == GROUP_KEYS and K_CHUNK == 2 * tq and L % K_CHUNK == 0
    n_sel = min(TOPK_MAX, L // 4)
    assert n_sel <= K_CHUNK
    body = functools.partial(_dsa_body, n_sel=n_sel, seq_len=L)
    q_spec = lambda w: pl.BlockSpec((1, tq, w), lambda b, i: (b, i, 0))
    feat_spec = lambda w: pl.BlockSpec((1, w, tq), lambda b, i: (b, 0, i))
    seq_spec = lambda a: pl.BlockSpec((1,) + a.shape[1:], lambda b, i: (b, 0, 0),
                                      pipeline_mode=pl.Buffered(1))
    n_groups = pl.cdiv(GROUPS_PER_CHUNK * (L // K_CHUNK), SEARCH_GROUPS) * SEARCH_GROUPS
    return pl.pallas_call(
        body,
        grid=(B, L // tq),
        in_specs=[q_spec(IDX_HEADS * IDX_DIM), feat_spec(MISC_W), seq_spec(ki),
                  q_spec(ATT_WIDTH), seq_spec(k), seq_spec(vt)],
        out_specs=feat_spec(ATT_WIDTH),
        out_shape=jax.ShapeDtypeStruct((B, ATT_WIDTH, L), F32),
        scratch_shapes=[
            pltpu.VMEM((L // K_CHUNK, K_CHUNK, tq), I32),
            pltpu.VMEM((WORD_BITS, n_groups, SUBLANES, tq), I32),
            pltpu.VMEM((n_groups, SUBLANES, tq), I32),
            pltpu.VMEM((IDX_HEADS, GROUP_KEYS, tq), F32), pltpu.VMEM((IDX_HEADS, GROUP_KEYS, tq), F32),
            pltpu.VMEM((ATT_HEADS, K_CHUNK, tq), F32), pltpu.VMEM((ATT_HEADS, K_CHUNK, tq), F32),
            pltpu.VMEM((ATT_HEADS, 1, tq), F32),
            pltpu.VMEM((ATT_HEADS, ATT_HEAD_DIM + PACKED_ROWS, tq), F32)],
        compiler_params=pltpu.CompilerParams(
            dimension_semantics=("parallel", "arbitrary"), vmem_limit_bytes=VMEM_LIMIT),
    )(qi, misct, ki, q, k, vt)


def _gla_body(gq_ref, gk_ref, gvt_ref, gv_ref, misc_ref, w2_ref, b2_ref, o_ref, st_ref):
    n, kw, vw = GLA_SUB, GLA_KW, GLA_WIDTH

    @pl.when(pl.program_id(1) == 0)
    def _():
        st_ref[...] = jnp.zeros_like(st_ref)

    rows = gq_ref.shape[1]
    grp = GLA_GROUP
    hv = lax.broadcasted_iota(I32, (vw, kw), 0) // GLA_DV
    hd = lax.broadcasted_iota(I32, (vw, kw), 1) // GLA_DK
    st_mask = hv == hd
    ed = lax.broadcasted_iota(I32, (kw, vw), 0) // GLA_DK
    ev = lax.broadcasted_iota(I32, (kw, vw), 1) // GLA_DV
    expand = jnp.where(ed == ev, 1.0, 0.0).astype(BF16)
    row_i = lax.broadcasted_iota(I32, (n, kw), 0)
    in_blk = lax.broadcasted_iota(I32, (grp, kw), 0) % n
    scale = GLA_DK ** -0.5

    def group(gi, carry):
        g0 = pl.multiple_of(gi * grp, grp)
        g_lr = misc_ref[0, pl.ds(g0, grp), :][:, G_LR_OFF:G_LR_OFF + GLA_GATE_RANK]
        z = jnp.dot(g_lr, w2_ref[...], precision=HIGHEST,
                    preferred_element_type=F32) + b2_ref[...]
        log_a = (jnp.minimum(z, 0.0) - jnp.log1p(jnp.exp(-jnp.abs(z)))) * (1.0 / GLA_TAU)
        b_all = log_a
        shift = 1
        while shift < n:
            b_all = b_all + jnp.where(in_blk >= shift, pltpu.roll(b_all, shift, axis=0), 0.0)
            shift *= 2
        q_all = gq_ref[0, pl.ds(g0, grp), :] * scale
        k_all = gk_ref[0, pl.ds(g0, grp), :]
        v_all = gv_ref[0, pl.ds(g0, grp), :]
        vt_all = gvt_ref[0, :, pl.ds(g0, grp)]
        for blk in range(grp // n):
            sl = slice(blk * n, (blk + 1) * n)
            _gla_block(q_all[sl], k_all[sl], v_all[sl], vt_all[:, sl], b_all[sl],
                       o_ref.at[0, pl.ds(g0 + blk * n, n), :], st_ref,
                       st_mask, expand, row_i)
        return carry

    lax.fori_loop(0, rows // grp, group, 0)


def _gla_block(qb, kb, vb, vtb, b, o_blk, st_ref, st_mask, expand, row_i):
    n, vw = GLA_SUB, GLA_WIDTH
    st = st_ref[...]
    o_inter = _dot_nt((qb * jnp.exp(b)).astype(BF16), st.astype(BF16))
    terms = []
    for j in range(n):
        diff = jnp.where(row_i >= j, b - b[j:j + 1, :], -jnp.inf)
        terms.append(qb * kb[j:j + 1, :] * jnp.exp(diff))
    t_all = jnp.concatenate(terms, axis=0).astype(BF16)
    g_all = _dot(t_all, expand)
    o_intra = jnp.zeros((n, vw), F32)
    for j in range(n):
        o_intra = o_intra + g_all[j * n:(j + 1) * n, :] * vb[j:j + 1, :]
    o_blk[...] = o_inter + o_intra
    b_last = b[n - 1:n, :]
    k_dec = (kb * jnp.exp(b_last - b)).astype(BF16)
    upd = _dot(vtb.astype(BF16), k_dec)
    st_ref[...] = st * jnp.exp(b_last) + jnp.where(st_mask, upd, 0.0)


def _gla(gq, gk, gvt, gv, misc, w_gate2, b_gate2):
    B, L, _ = gq.shape
    rows = min(GLA_ROWS, L)
    row_spec = lambda w: pl.BlockSpec((1, rows, w), lambda b, i: (b, i, 0))
    full = lambda a: pl.BlockSpec(a.shape, lambda b, i: (0,) * a.ndim)
    return pl.pallas_call(
        _gla_body,
        grid=(B, L // rows),
        in_specs=[row_spec(GLA_KW), row_spec(GLA_KW),
                  pl.BlockSpec((1, GLA_WIDTH, rows), lambda b, i: (b, 0, i)),
                  row_spec(GLA_WIDTH), row_spec(MISC_W), full(w_gate2), full(b_gate2)],
        out_specs=row_spec(GLA_WIDTH),
        out_shape=jax.ShapeDtypeStruct((B, L, GLA_WIDTH), F32),
        scratch_shapes=[pltpu.VMEM((GLA_WIDTH, GLA_KW), F32)],
        compiler_params=pltpu.CompilerParams(
            dimension_semantics=("parallel", "arbitrary"), vmem_limit_bytes=VMEM_LIMIT),
    )(gq, gk, gvt, gv, misc, w_gate2, b_gate2)


def _rms(x, g):
    ms = jnp.mean(x * x, axis=-1, keepdims=True)
    return (x * lax.rsqrt(ms + NORM_EPS)) * g


def _outproj_body(x_ref, attt_ref, gla_ref, go_ref, g1_ref, ag_ref, gg_ref, wo_ref, o_ref):
    a = _rms(attt_ref[0].T, ag_ref[...])
    gl = gla_ref[0]
    parts = []
    for h in range(GLA_HEADS):
        sl = slice(h * GLA_DV, (h + 1) * GLA_DV)
        parts.append(_rms(gl[:, sl], gg_ref[...]))
    g = jnp.concatenate(parts, axis=1) * _silu(go_ref[0])
    mix = (_dot(a.astype(BF16), wo_ref[0:ATT_WIDTH, :])
           + _dot(g.astype(BF16), wo_ref[ATT_WIDTH:, :]))
    o_ref[0] = x_ref[0] + g1_ref[0] * mix


def _output_projection(x, att, gla, go, g1, att_out_g, gla_out_g, w_out):
    B, L, D = x.shape
    tm = min(ROWS_PROJ, L)
    row_spec = lambda w: pl.BlockSpec((1, tm, w), lambda b, i: (b, i, 0))
    full = lambda a: pl.BlockSpec(a.shape, lambda b, i: (0,) * a.ndim)
    vec = pl.BlockSpec((1, 1, D), lambda b, i: (b, 0, 0))
    wo = w_out.astype(BF16)
    return pl.pallas_call(
        _outproj_body,
        grid=(B, L // tm),
        in_specs=[row_spec(D), pl.BlockSpec((1, ATT_WIDTH, tm), lambda b, i: (b, 0, i)),
                  row_spec(GLA_WIDTH), row_spec(GLA_WIDTH),
                  vec, full(att_out_g), full(gla_out_g), full(wo)],
        out_specs=row_spec(D),
        out_shape=jax.ShapeDtypeStruct((B, L, D), F32),
        compiler_params=pltpu.CompilerParams(
            dimension_semantics=("parallel", "parallel"), vmem_limit_bytes=VMEM_LIMIT),
    )(x, att, gla, go, g1, att_out_g, gla_out_g, wo)


def _ffn_body(x_ref, sc_ref, sh_ref, g2_ref, ng_ref, fg_ref, wgu_ref, wd_ref, o_ref, *, d_ff, final):
    x = x_ref[0]
    h = _rms(x, ng_ref[...]) * (1.0 + sc_ref[0]) + sh_ref[0]
    gu = _dot(h.astype(BF16), wgu_ref[...])
    act = _silu(gu[:, :d_ff]) * gu[:, d_ff:]
    y = x + g2_ref[0] * _dot(act.astype(BF16), wd_ref[...])
    o_ref[0] = _rms(y, fg_ref[...]) if final else y


def _ffn(x, sc2, sh2, g2, norm2_g, final_g, w_gate_up, w_down, final):
    B, L, D = x.shape
    d_ff = w_down.shape[0]
    tm = min(ROWS_FFN, L)
    row_spec = pl.BlockSpec((1, tm, D), lambda b, i: (b, i, 0))
    full = lambda a: pl.BlockSpec(a.shape, lambda b, i: (0,) * a.ndim,
                                  pipeline_mode=pl.Buffered(1))
    vec = pl.BlockSpec((1, 1, D), lambda b, i: (b, 0, 0))
    wgu = w_gate_up.astype(BF16)
    wd = w_down.astype(BF16)
    return pl.pallas_call(
        functools.partial(_ffn_body, d_ff=d_ff, final=final),
        grid=(B, L // tm),
        in_specs=[row_spec, vec, vec, vec, full(norm2_g), full(final_g), full(wgu), full(wd)],
        out_specs=row_spec,
        out_shape=jax.ShapeDtypeStruct((B, L, D), F32),
        compiler_params=pltpu.CompilerParams(
            dimension_semantics=("parallel", "parallel"), vmem_limit_bytes=VMEM_LIMIT),
    )(x, sc2, sh2, g2, norm2_g, final_g, wgu, wd)


def kernel(x, c, w_mod, b_mod, norm1_g, w_in, w_gate2, b_gate2, att_out_g, gla_out_g,
           w_out, norm2_g, w_gate_up, w_down, final_g):
    B, L, D = x.shape
    depth = w_mod.shape[0]
    for l in range(depth):
        mod = _modulation(c, w_mod[l], b_mod[l][None, :])
        sh1, sc1, g1, sh2, sc2, g2 = [m[:, None, :] for m in jnp.split(mod, N_MOD, axis=-1)]
        q, k, qi, misc, ki, gq, gk, gv, go, vt, gvt, misct = _input_projection(
            x, sc1, sh1, norm1_g[l][None, :], w_in[l])
        att_t = _dsa_attention(q, k, vt, qi, ki, misct)
        gla = _gla(gq, gk, gvt, gv, misc, w_gate2[l], b_gate2[l][None, :])
        x = _output_projection(x, att_t, gla, go, g1, att_out_g[l][None, :],
                               gla_out_g[l][None, :], w_out[l])
        x = _ffn(x, sc2, sh2, g2, norm2_g[l][None, :], final_g[None, :], w_gate_up[l],
                 w_down[l], final=(l == depth - 1))
    return x
```

```python
import functools

import jax
import jax.numpy as jnp
from jax import lax
from jax.experimental import pallas as pl
from jax.experimental.pallas import tpu as pltpu

F32 = jnp.float32
BF16 = jnp.bfloat16
I32 = jnp.int32
HIGHEST = lax.Precision.HIGHEST

NORM_EPS = 1e-6
ATT_HEADS = 4
ATT_HEAD_DIM = 128
ATT_WIDTH = ATT_HEADS * ATT_HEAD_DIM
IDX_HEADS = 8
IDX_DIM = 64
TOPK_MAX = 256
GLA_HEADS = 4
GLA_DK = 64
GLA_DV = 128
GLA_KW = GLA_HEADS * GLA_DK
GLA_WIDTH = GLA_HEADS * GLA_DV
GLA_GATE_RANK = 16
GLA_TAU = 16.0
N_MOD = 6

LANES = 128
SUBLANES = 8
PACKED_ROWS = 16
MISC_W = LANES
W_IDX_OFF = IDX_DIM
G_LR_OFF = IDX_DIM + IDX_HEADS

INT_MIN = -2 ** 31
KEY_MASKED = INT_MIN
MASKED_LOGIT = -1e30
LOG2_E = 1.4426950408889634
VMEM_LIMIT = 56 * 1024 * 1024

ROWS_PROJ = 512
ROWS_FFN = 256
Q_TILE = 256
K_CHUNK = 512
WORD_BITS = 32
GROUP_KEYS = WORD_BITS * SUBLANES
GROUPS_PER_CHUNK = K_CHUNK // GROUP_KEYS
SEARCH_GROUPS = 8
GLA_SUB = 16
GLA_GROUP = 128
GLA_ROWS = 512


def _dot(a, b):
    return jnp.dot(a, b, preferred_element_type=F32)


def _dot_nt(a, b):
    return lax.dot_general(a, b, (((1,), (1,)), ((), ())), preferred_element_type=F32)


def _silu(x):
    return x * jax.nn.sigmoid(x)


def _mod_body(c_ref, w_ref, b_ref, o_ref):
    s = _silu(c_ref[...])
    o_ref[...] = jnp.dot(s, w_ref[...], precision=HIGHEST,
                         preferred_element_type=F32) + b_ref[...]


def _modulation(c, w_mod, b_mod):
    B, D = c.shape
    N = w_mod.shape[1]
    tn = D
    return pl.pallas_call(
        _mod_body,
        grid=(N // tn,),
        in_specs=[pl.BlockSpec((B, D), lambda j: (0, 0)),
                  pl.BlockSpec((D, tn), lambda j: (0, j)),
                  pl.BlockSpec((1, tn), lambda j: (0, j))],
        out_specs=pl.BlockSpec((B, tn), lambda j: (0, j)),
        out_shape=jax.ShapeDtypeStruct((B, N), F32),
    )(c, w_mod, b_mod)


def _inproj_body(x_ref, sc_ref, sh_ref, g_ref, wa_ref, wm_ref, wg_ref, wt_ref,
                 qt_o, k_o, qit_o, misc_o, ki_o, gq_o, gk_o, gv_o, go_o, vt_o, gvt_o, misct_o):
    x = x_ref[0]
    ms = jnp.mean(x * x, axis=-1, keepdims=True)
    y = (x * lax.rsqrt(ms + NORM_EPS)) * g_ref[...]
    h = y * (1.0 + sc_ref[0]) + sh_ref[0]
    hb = h.astype(BF16)
    w = ATT_WIDTH
    k_o[0] = _dot(hb, wa_ref[...]).astype(BF16)
    pm = _dot(hb, wm_ref[...])
    misc_o[0] = pm
    ki_o[0] = pm[:, 0:IDX_DIM].astype(BF16)
    pg = _dot(hb, wg_ref[...])
    gq_o[0] = pg[:, 0:GLA_KW]
    gk_o[0] = pg[:, GLA_KW:2 * GLA_KW]
    gv_o[0] = pg[:, 2 * GLA_KW:2 * GLA_KW + GLA_WIDTH]
    go_o[0] = pg[:, 2 * GLA_KW + GLA_WIDTH:]
    pt = _dot_nt(wt_ref[...], hb)
    qt_o[0] = (pt[0:w, :] * (ATT_HEAD_DIM ** -0.5 * LOG2_E)).astype(BF16)
    qit_o[0] = pt[w:2 * w, :].astype(BF16)
    vt_o[0] = pt[2 * w:3 * w, :].astype(BF16)
    gvt_o[0] = pt[3 * w:3 * w + GLA_WIDTH, :]
    misct_o[0] = pt[3 * w + GLA_WIDTH:, :]


def _input_projection(x, sc1, sh1, norm1_g, w_in):
    B, L, D = x.shape
    tm = min(ROWS_PROJ, L)
    o = [0]
    for s in (ATT_WIDTH, ATT_WIDTH, ATT_WIDTH, IDX_HEADS * IDX_DIM, IDX_DIM, IDX_HEADS,
              GLA_KW, GLA_KW, GLA_WIDTH, GLA_WIDTH, GLA_GATE_RANK):
        o.append(o[-1] + s)
    col = lambda i: w_in[:, o[i]:o[i + 1]]
    wa = col(1).astype(BF16)
    pad = jnp.zeros((D, MISC_W - IDX_DIM - IDX_HEADS - GLA_GATE_RANK), w_in.dtype)
    wm = jnp.concatenate([col(4), col(5), col(10), pad], axis=1).astype(BF16)
    wg = jnp.concatenate([col(6), col(7), col(8), col(9)], axis=1).astype(BF16)
    wt = jnp.concatenate([col(0), col(3), col(2), col(8), wm], axis=1).T.astype(BF16)

    row = lambda w, dt: jax.ShapeDtypeStruct((B, L, w), dt)
    feat = lambda w, dt: jax.ShapeDtypeStruct((B, w, L), dt)
    row_spec = lambda w: pl.BlockSpec((1, tm, w), lambda b, i: (b, i, 0))
    feat_spec = lambda w: pl.BlockSpec((1, w, tm), lambda b, i: (b, 0, i))
    full = lambda a: pl.BlockSpec(a.shape, lambda b, i: (0,) * a.ndim,
                                  pipeline_mode=pl.Buffered(1))
    vec = pl.BlockSpec((1, 1, D), lambda b, i: (b, 0, 0))
    return pl.pallas_call(
        _inproj_body,
        grid=(B, L // tm),
        in_specs=[row_spec(D), vec, vec, full(norm1_g), full(wa), full(wm), full(wg), full(wt)],
        out_specs=[feat_spec(ATT_WIDTH), row_spec(ATT_WIDTH),
                   feat_spec(IDX_HEADS * IDX_DIM), row_spec(MISC_W), row_spec(IDX_DIM),
                   row_spec(GLA_KW), row_spec(GLA_KW), row_spec(GLA_WIDTH), row_spec(GLA_WIDTH),
                   feat_spec(ATT_WIDTH), feat_spec(GLA_WIDTH), feat_spec(MISC_W)],
        out_shape=[feat(ATT_WIDTH, BF16), row(ATT_WIDTH, BF16),
                   feat(IDX_HEADS * IDX_DIM, BF16), row(MISC_W, F32), row(IDX_DIM, BF16),
                   row(GLA_KW, F32), row(GLA_KW, F32), row(GLA_WIDTH, F32), row(GLA_WIDTH, F32),
                   feat(ATT_WIDTH, BF16), feat(GLA_WIDTH, F32), feat(MISC_W, F32)],
        compiler_params=pltpu.CompilerParams(
            dimension_semantics=("parallel", "parallel"), vmem_limit_bytes=VMEM_LIMIT),
    )(x, sc1, sh1, norm1_g, wa, wm, wg, wt)


def _bit_transpose(words):
    x = list(words)
    for shift, mask in ((16, 0x0000FFFF), (8, 0x00FF00FF), (4, 0x0F0F0F0F),
                        (2, 0x33333333), (1, 0x55555555)):
        for lo in range(WORD_BITS):
            if lo & shift == 0:
                a, b = x[lo], x[lo + shift]
                t = (a ^ lax.shift_right_logical(b, shift)) & mask
                x[lo] = a ^ t
                x[lo + shift] = b ^ lax.shift_left(t, shift)
    return x


def _dsa_body(qit_ref, misct_ref, ki_ref, qt_ref, k_ref, vt_ref, o_ref,
              keys_ref, planes_ref, cand_ref, dta_ref, dtb_ref, qka_ref, qkb_ref, m_ref, acc_ref,
              *, n_sel, seq_len):
    i = pl.program_id(1)
    tq, ck = Q_TILE, K_CHUNK
    n_full = (i * tq) // ck
    n_chunks = n_full + 1
    idx_scale = (IDX_HEADS ** -0.5) * (IDX_DIM ** -0.5)

    s_iota = lax.broadcasted_iota(I32, (ck, tq), 0)
    t_pos = i * tq + lax.broadcasted_iota(I32, (ck, tq), 1)
    w = misct_ref[0][W_IDX_OFF:W_IDX_OFF + IDX_HEADS, :] * idx_scale
    qit_heads = [qit_ref[0, h * IDX_DIM:(h + 1) * IDX_DIM, :] for h in range(IDX_HEADS)]
    w_rows = [w[h:h + 1, :] for h in range(IDX_HEADS)]

    half = GROUP_KEYS
    s_half = lax.broadcasted_iota(I32, (half, tq), 0)
    t_half = i * tq + lax.broadcasted_iota(I32, (half, tq), 1)

    def dots_to(buf, u):
        kc = ki_ref[0, pl.ds(pl.multiple_of(u * half, half), half), :]
        for h in range(IDX_HEADS):
            buf[h] = _dot(kc, qit_heads[h])

    def keys_from(buf, c, second, diagonal):
        u = 2 * c + second
        score = jnp.zeros((half, tq), F32)
        for h in range(IDX_HEADS):
            score = score + jnp.maximum(buf[h], 0.0) * w_rows[h]
        bits = lax.bitcast_convert_type(score, I32)
        key = jnp.where(bits < 0, INT_MIN - bits, bits)
        if diagonal:
            key = jnp.where(u * half + s_half <= t_half, key, KEY_MASKED)
        keys_ref[c, second * half:(second + 1) * half, :] = key

    dots_to(dta_ref, 0)

    def score_pair(c, carry):
        dots_to(dtb_ref, 2 * c + 1)
        keys_from(dta_ref, c, 0, diagonal=False)
        dots_to(dta_ref, 2 * c + 2)
        keys_from(dtb_ref, c, 1, diagonal=False)
        return carry

    lax.fori_loop(0, n_full, score_pair, 0)
    n_halves = GROUPS_PER_CHUNK * n_chunks

    @pl.when(i % 2 == 0)
    def _():
        keys_from(dta_ref, n_full, 0, diagonal=True)
        keys_ref[n_full, half:, :] = jnp.full((half, tq), KEY_MASKED, I32)

    @pl.when(i % 2 == 1)
    def _():
        dots_to(dtb_ref, i)
        keys_from(dta_ref, n_full, 0, diagonal=False)
        keys_from(dtb_ref, n_full, 1, diagonal=True)

    def slice_chunk(c, carry):
        for second in range(GROUPS_PER_CHUNK):
            u = GROUPS_PER_CHUNK * c + second
            for lb in range(tq // LANES):
                lanes = slice(lb * LANES, (lb + 1) * LANES)
                rows = [keys_ref[c, second * half + m * SUBLANES:
                                 second * half + (m + 1) * SUBLANES, lanes] ^ INT_MIN
                        for m in range(WORD_BITS)]
                for p, plane in enumerate(_bit_transpose(rows)):
                    planes_ref[p, u, :, lanes] = plane
            cand_ref[u] = jnp.full((SUBLANES, tq), -1, I32)
        return carry

    lax.fori_loop(0, n_chunks, slice_chunk, 0)

    sg = SEARCH_GROUPS
    n_steps = (n_halves + sg - 1) // sg

    def pad_group(g, carry):
        planes_ref[:, g] = jnp.zeros((WORD_BITS, SUBLANES, tq), I32)
        cand_ref[g] = jnp.zeros((SUBLANES, tq), I32)
        return carry

    lax.fori_loop(n_halves, sg * n_steps, pad_group, 0)

    def count_over_groups(words_of):
        def body(step, acc):
            words = words_of(pl.multiple_of(step * sg, sg))
            return acc + jnp.sum(lax.population_count(words).astype(F32), axis=0)
        acc = lax.fori_loop(0, n_steps, body, jnp.zeros((SUBLANES, tq), F32))
        return jnp.sum(acc, axis=0, keepdims=True)

    def decide(cnt, rank, kth_u):
        take = cnt >= rank
        return (jnp.where(take, 0, -1), jnp.where(take, rank, rank - cnt),
                lax.shift_left(kth_u, 1) | jnp.where(take, 1, 0))

    carry = decide(
        count_over_groups(lambda g0: cand_ref[pl.ds(g0, sg)] & planes_ref[0, pl.ds(g0, sg)]),
        jnp.full((1, tq), float(n_sel), F32), jnp.zeros((1, tq), I32))

    def search_pass(p, carry):
        flip, rank, kth_u = carry

        def candidates_with_bit(g0):
            gs = pl.ds(g0, sg)
            cand = cand_ref[gs] & (planes_ref[p - 1, gs] ^ flip)
            cand_ref[gs] = cand
            return cand & planes_ref[p, gs]

        return decide(count_over_groups(candidates_with_bit), rank, kth_u)

    flip, rank, kth_u = lax.fori_loop(1, WORD_BITS, search_pass, carry)

    def equal_to_threshold(g0):
        gs = pl.ds(g0, sg)
        cand = cand_ref[gs] & (planes_ref[WORD_BITS - 1, gs] ^ flip)
        cand_ref[gs] = cand
        return cand

    n_equal = count_over_groups(equal_to_threshold)
    kth = kth_u ^ INT_MIN
    has_tie = (n_equal > rank) & (kth > KEY_MASKED)
    kth = jnp.maximum(kth, KEY_MASKED + 1)

    @pl.when(jnp.max(jnp.where(has_tie, 1.0, 0.0)) > 0.0)
    def _():
        grp = lax.broadcasted_iota(I32, (sg, SUBLANES, tq), 0)
        sub = lax.broadcasted_iota(I32, (sg, SUBLANES, tq), 1)
        sub_bits = SUBLANES.bit_length() - 1

        def ties_before(bound):
            def words_of(g0):
                first = (g0 + grp) * GROUP_KEYS + sub
                n_low = lax.shift_right_arithmetic(bound - first + (SUBLANES - 1), sub_bits)
                n_low = jnp.clip(n_low, 0, WORD_BITS)
                mask = jnp.where(n_low > 0,
                                 lax.shift_left(jnp.int32(-1), WORD_BITS - jnp.maximum(n_low, 1)), 0)
                return cand_ref[pl.ds(g0, sg)] & mask
            return count_over_groups(words_of)

        n_bits = (seq_len - 1).bit_length()

        def idx_step(it, lo):
            cand = lo + lax.shift_left(jnp.int32(1), n_bits - 1 - it)
            return jnp.where(ties_before(cand) < rank, cand, lo)

        last = lax.fori_loop(0, n_bits, idx_step, jnp.zeros((1, tq), I32))
        last_kept = jnp.where(has_tie, last, seq_len)

        def demote(c, carry):
            kk = keys_ref[c]
            s_pos = c * ck + s_iota
            drop = (kk == kth) & (s_pos > last_kept)
            keys_ref[c] = jnp.where(drop, kk - 1, kk)
            return carry

        lax.fori_loop(0, n_chunks, demote, 0)

    hd = ATT_HEAD_DIM
    qt_heads = [qt_ref[0, h * hd:(h + 1) * hd, :] for h in range(ATT_HEADS)]
    ones_rows = jnp.ones((PACKED_ROWS, ck), BF16)

    heads = [slice(h * hd, (h + 1) * hd) for h in range(ATT_HEADS)]

    def qk_to(buf, c):
        kc = k_ref[0, pl.ds(pl.multiple_of(c * ck, ck), ck), :]
        for h, sl in enumerate(heads):
            buf[h] = _dot(kc[:, sl], qt_heads[h])

    def softmax_pv(c, buf):
        start = pl.multiple_of(c * ck, ck)
        bias = jnp.where(keys_ref[c] >= kth, 0.0, MASKED_LOGIT)
        logits = [buf[h] + bias for h in range(ATT_HEADS)]
        ms = [m_ref[h] for h in range(ATT_HEADS)]
        new_ms = [jnp.maximum(ms[h], jnp.max(logits[h], axis=0, keepdims=True))
                  for h in range(ATT_HEADS)]
        ps = [jnp.exp2(logits[h] - new_ms[h]).astype(BF16) for h in range(ATT_HEADS)]
        for h, sl in enumerate(heads):
            vt_ext = jnp.concatenate([vt_ref[0, sl, pl.ds(start, ck)], ones_rows], axis=0)
            acc_ref[h] = jnp.exp2(ms[h] - new_ms[h]) * acc_ref[h] + _dot(vt_ext, ps[h])
            m_ref[h] = new_ms[h]

    m_ref[...] = jnp.full_like(m_ref, MASKED_LOGIT)
    acc_ref[...] = jnp.zeros_like(acc_ref)
    qk_to(qka_ref, 0)

    def attend_pair(j, carry):
        c = 2 * j
        qk_to(qkb_ref, c + 1)
        softmax_pv(c, qka_ref)
        qk_to(qka_ref, c + 2)
        softmax_pv(c + 1, qkb_ref)
        return carry

    lax.fori_loop(0, n_full // 2, attend_pair, 0)

    @pl.when(n_full % 2 == 0)
    def _():
        softmax_pv(n_full, qka_ref)

    @pl.when(n_full % 2 == 1)
    def _():
        qk_to(qkb_ref, n_full)
        softmax_pv(n_full - 1, qka_ref)
        softmax_pv(n_full, qkb_ref)

    for h in range(ATT_HEADS):
        o_ref[0, h * hd:(h + 1) * hd, :] = acc_ref[h, :hd, :] / acc_ref[h, hd:hd + 1, :]


def _dsa_attention(qt, k, vt, qit, ki, misct):
    B, L, _ = k.shape
    tq = Q_TILE
    assert tq == GROUP_KEYS and K_CHUNK == 2 * tq and L % K_CHUNK == 0
    n_sel = min(TOPK_MAX, L // 4)
    assert n_sel <= K_CHUNK
    body = functools.partial(_dsa_body, n_sel=n_sel, seq_len=L)
    feat_spec = lambda w: pl.BlockSpec((1, w, tq), lambda b, i: (b, 0, i))
    seq_spec = lambda a: pl.BlockSpec((1,) + a.shape[1:], lambda b, i: (b, 0, 0),
                                      pipeline_mode=pl.Buffered(1))
    n_groups = pl.cdiv(GROUPS_PER_CHUNK * (L // K_CHUNK), SEARCH_GROUPS) * SEARCH_GROUPS
    return pl.pallas_call(
        body,
        grid=(B, L // tq),
        in_specs=[feat_spec(IDX_HEADS * IDX_DIM), feat_spec(MISC_W), seq_spec(ki),
                  feat_spec(ATT_WIDTH), seq_spec(k), seq_spec(vt)],
        out_specs=feat_spec(ATT_WIDTH),
        out_shape=jax.ShapeDtypeStruct((B, ATT_WIDTH, L), F32),
        scratch_shapes=[
            pltpu.VMEM((L // K_CHUNK, K_CHUNK, tq), I32),
            pltpu.VMEM((WORD_BITS, n_groups, SUBLANES, tq), I32),
            pltpu.VMEM((n_groups, SUBLANES, tq), I32),
            pltpu.VMEM((IDX_HEADS, GROUP_KEYS, tq), F32), pltpu.VMEM((IDX_HEADS, GROUP_KEYS, tq), F32),
            pltpu.VMEM((ATT_HEADS, K_CHUNK, tq), F32), pltpu.VMEM((ATT_HEADS, K_CHUNK, tq), F32),
            pltpu.VMEM((ATT_HEADS, 1, tq), F32),
            pltpu.VMEM((ATT_HEADS, ATT_HEAD_DIM + PACKED_ROWS, tq), F32)],
        compiler_params=pltpu.CompilerParams(
            dimension_semantics=("parallel", "arbitrary"), vmem_limit_bytes=VMEM_LIMIT),
    )(qit, misct, ki, qt, k, vt)


def _gla_body(gq_ref, gk_ref, gvt_ref, gv_ref, misc_ref, w2_ref, b2_ref, o_ref, st_ref):
    n, kw, vw = GLA_SUB, GLA_KW, GLA_WIDTH

    @pl.when(pl.program_id(1) == 0)
    def _():
        st_ref[...] = jnp.zeros_like(st_ref)

    rows = gq_ref.shape[1]
    grp = GLA_GROUP
    hv = lax.broadcasted_iota(I32, (vw, kw), 0) // GLA_DV
    hd = lax.broadcasted_iota(I32, (vw, kw), 1) // GLA_DK
    st_mask = hv == hd
    ed = lax.broadcasted_iota(I32, (kw, vw), 0) // GLA_DK
    ev = lax.broadcasted_iota(I32, (kw, vw), 1) // GLA_DV
    expand = jnp.where(ed == ev, 1.0, 0.0).astype(BF16)
    row_i = lax.broadcasted_iota(I32, (n, kw), 0)
    in_blk = lax.broadcasted_iota(I32, (grp, kw), 0) % n
    scale = GLA_DK ** -0.5

    def group(gi, carry):
        g0 = pl.multiple_of(gi * grp, grp)
        g_lr = misc_ref[0, pl.ds(g0, grp), :][:, G_LR_OFF:G_LR_OFF + GLA_GATE_RANK]
        z = jnp.dot(g_lr, w2_ref[...], precision=HIGHEST,
                    preferred_element_type=F32) + b2_ref[...]
        log_a = (jnp.minimum(z, 0.0) - jnp.log1p(jnp.exp(-jnp.abs(z)))) * (1.0 / GLA_TAU)
        b_all = log_a
        shift = 1
        while shift < n:
            b_all = b_all + jnp.where(in_blk >= shift, pltpu.roll(b_all, shift, axis=0), 0.0)
            shift *= 2
        q_all = gq_ref[0, pl.ds(g0, grp), :] * scale
        k_all = gk_ref[0, pl.ds(g0, grp), :]
        v_all = gv_ref[0, pl.ds(g0, grp), :]
        vt_all = gvt_ref[0, :, pl.ds(g0, grp)]
        for blk in range(grp // n):
            sl = slice(blk * n, (blk + 1) * n)
            _gla_block(q_all[sl], k_all[sl], v_all[sl], vt_all[:, sl], b_all[sl],
                       o_ref.at[0, pl.ds(g0 + blk * n, n), :], st_ref,
                       st_mask, expand, row_i)
        return carry

    lax.fori_loop(0, rows // grp, group, 0)


def _gla_block(qb, kb, vb, vtb, b, o_blk, st_ref, st_mask, expand, row_i):
    n, vw = GLA_SUB, GLA_WIDTH
    st = st_ref[...]
    o_inter = _dot_nt((qb * jnp.exp(b)).astype(BF16), st.astype(BF16))
    terms = []
    for j in range(n):
        diff = jnp.where(row_i >= j, b - b[j:j + 1, :], -jnp.inf)
        terms.append(qb * kb[j:j + 1, :] * jnp.exp(diff))
    t_all = jnp.concatenate(terms, axis=0).astype(BF16)
    g_all = _dot(t_all, expand)
    o_intra = jnp.zeros((n, vw), F32)
    for j in range(n):
        o_intra = o_intra + g_all[j * n:(j + 1) * n, :] * vb[j:j + 1, :]
    o_blk[...] = o_inter + o_intra
    b_last = b[n - 1:n, :]
    k_dec = (kb * jnp.exp(b_last - b)).astype(BF16)
    upd = _dot(vtb.astype(BF16), k_dec)
    st_ref[...] = st * jnp.exp(b_last) + jnp.where(st_mask, upd, 0.0)


def _gla(gq, gk, gvt, gv, misc, w_gate2, b_gate2):
    B, L, _ = gq.shape
    rows = min(GLA_ROWS, L)
    row_spec = lambda w: pl.BlockSpec((1, rows, w), lambda b, i: (b, i, 0))
    full = lambda a: pl.BlockSpec(a.shape, lambda b, i: (0,) * a.ndim)
    return pl.pallas_call(
        _gla_body,
        grid=(B, L // rows),
        in_specs=[row_spec(GLA_KW), row_spec(GLA_KW),
                  pl.BlockSpec((1, GLA_WIDTH, rows), lambda b, i: (b, 0, i)),
                  row_spec(GLA_WIDTH), row_spec(MISC_W), full(w_gate2), full(b_gate2)],
        out_specs=row_spec(GLA_WIDTH),
        out_shape=jax.ShapeDtypeStruct((B, L, GLA_WIDTH), F32),
        scratch_shapes=[pltpu.VMEM((GLA_WIDTH, GLA_KW), F32)],
        compiler_params=pltpu.CompilerParams(
            dimension_semantics=("parallel", "arbitrary"), vmem_limit_bytes=VMEM_LIMIT),
    )(gq, gk, gvt, gv, misc, w_gate2, b_gate2)


def _rms(x, g):
    ms = jnp.mean(x * x, axis=-1, keepdims=True)
    return (x * lax.rsqrt(ms + NORM_EPS)) * g


def _outproj_body(x_ref, attt_ref, gla_ref, go_ref, g1_ref, ag_ref, gg_ref, wo_ref, o_ref):
    a = _rms(attt_ref[0].T, ag_ref[...])
    gl = gla_ref[0]
    parts = []
    for h in range(GLA_HEADS):
        sl = slice(h * GLA_DV, (h + 1) * GLA_DV)
        parts.append(_rms(gl[:, sl], gg_ref[...]))
    g = jnp.concatenate(parts, axis=1) * _silu(go_ref[0])
    mix = (_dot(a.astype(BF16), wo_ref[0:ATT_WIDTH, :])
           + _dot(g.astype(BF16), wo_ref[ATT_WIDTH:, :]))
    o_ref[0] = x_ref[0] + g1_ref[0] * mix


def _output_projection(x, att, gla, go, g1, att_out_g, gla_out_g, w_out):
    B, L, D = x.shape
    tm = min(ROWS_PROJ, L)
    row_spec = lambda w: pl.BlockSpec((1, tm, w), lambda b, i: (b, i, 0))
    full = lambda a: pl.BlockSpec(a.shape, lambda b, i: (0,) * a.ndim)
    vec = pl.BlockSpec((1, 1, D), lambda b, i: (b, 0, 0))
    wo = w_out.astype(BF16)
    return pl.pallas_call(
        _outproj_body,
        grid=(B, L // tm),
        in_specs=[row_spec(D), pl.BlockSpec((1, ATT_WIDTH, tm), lambda b, i: (b, 0, i)),
                  row_spec(GLA_WIDTH), row_spec(GLA_WIDTH),
                  vec, full(att_out_g), full(gla_out_g), full(wo)],
        out_specs=row_spec(D),
        out_shape=jax.ShapeDtypeStruct((B, L, D), F32),
        compiler_params=pltpu.CompilerParams(
            dimension_semantics=("parallel", "parallel"), vmem_limit_bytes=VMEM_LIMIT),
    )(x, att, gla, go, g1, att_out_g, gla_out_g, wo)


def _ffn_body(x_ref, sc_ref, sh_ref, g2_ref, ng_ref, fg_ref, wgu_ref, wd_ref, o_ref, *, d_ff, final):
    x = x_ref[0]
    h = _rms(x, ng_ref[...]) * (1.0 + sc_ref[0]) + sh_ref[0]
    gu = _dot(h.astype(BF16), wgu_ref[...])
    act = _silu(gu[:, :d_ff]) * gu[:, d_ff:]
    y = x + g2_ref[0] * _dot(act.astype(BF16), wd_ref[...])
    o_ref[0] = _rms(y, fg_ref[...]) if final else y


def _ffn(x, sc2, sh2, g2, norm2_g, final_g, w_gate_up, w_down, final):
    B, L, D = x.shape
    d_ff = w_down.shape[0]
    tm = min(ROWS_FFN, L)
    row_spec = pl.BlockSpec((1, tm, D), lambda b, i: (b, i, 0))
    full = lambda a: pl.BlockSpec(a.shape, lambda b, i: (0,) * a.ndim,
                                  pipeline_mode=pl.Buffered(1))
    vec = pl.BlockSpec((1, 1, D), lambda b, i: (b, 0, 0))
    wgu = w_gate_up.astype(BF16)
    wd = w_down.astype(BF16)
    return pl.pallas_call(
        functools.partial(_ffn_body, d_ff=d_ff, final=final),
        grid=(B, L // tm),
        in_specs=[row_spec, vec, vec, vec, full(norm2_g), full(final_g), full(wgu), full(wd)],
        out_specs=row_spec,
        out_shape=jax.ShapeDtypeStruct((B, L, D), F32),
        compiler_params=pltpu.CompilerParams(
            dimension_semantics=("parallel", "parallel"), vmem_limit_bytes=VMEM_LIMIT),
    )(x, sc2, sh2, g2, norm2_g, final_g, wgu, wd)


def kernel(x, c, w_mod, b_mod, norm1_g, w_in, w_gate2, b_gate2, att_out_g, gla_out_g,
           w_out, norm2_g, w_gate_up, w_down, final_g):
    B, L, D = x.shape
    depth = w_mod.shape[0]
    for l in range(depth):
        mod = _modulation(c, w_mod[l], b_mod[l][None, :])
        sh1, sc1, g1, sh2, sc2, g2 = [m[:, None, :] for m in jnp.split(mod, N_MOD, axis=-1)]
        qt, k, qit, misc, ki, gq, gk, gv, go, vt, gvt, misct = _input_projection(
            x, sc1, sh1, norm1_g[l][None, :], w_in[l])
        att_t = _dsa_attention(qt, k, vt, qit, ki, misct)
        gla = _gla(gq, gk, gvt, gv, misc, w_gate2[l], b_gate2[l][None, :])
        x = _output_projection(x, att_t, gla, go, g1, att_out_g[l][None, :],
                               gla_out_g[l][None, :], w_out[l])
        x = _ffn(x, sc2, sh2, g2, norm2_g[l][None, :], final_g[None, :], w_gate_up[l],
                 w_down[l], final=(l == depth - 1))
    return x
```

```python
import functools

import jax
import jax.numpy as jnp
from jax import lax
from jax.experimental import pallas as pl
from jax.experimental.pallas import tpu as pltpu

F32 = jnp.float32
BF16 = jnp.bfloat16
I32 = jnp.int32
HIGHEST = lax.Precision.HIGHEST

NORM_EPS = 1e-6
ATT_HEADS = 4
ATT_HEAD_DIM = 128
ATT_WIDTH = ATT_HEADS * ATT_HEAD_DIM
IDX_HEADS = 8
IDX_DIM = 64
TOPK_MAX = 256
GLA_HEADS = 4
GLA_DK = 64
GLA_DV = 128
GLA_KW = GLA_HEADS * GLA_DK
GLA_WIDTH = GLA_HEADS * GLA_DV
GLA_GATE_RANK = 16
GLA_TAU = 16.0
N_MOD = 6

LANES = 128
SUBLANES = 8
PACKED_ROWS = 16
MISC_W = LANES
W_IDX_OFF = IDX_DIM
G_LR_OFF = IDX_DIM + IDX_HEADS

INT_MIN = -2 ** 31
KEY_MASKED = INT_MIN
MASKED_LOGIT = -1e30
LOG2_E = 1.4426950408889634
VMEM_LIMIT = 56 * 1024 * 1024

ROWS_PROJ = 512
ROWS_FFN = 256
Q_TILE = 256
K_CHUNK = 512
WORD_BITS = 32
GROUP_KEYS = WORD_BITS * SUBLANES
GROUPS_PER_CHUNK = K_CHUNK // GROUP_KEYS
SEARCH_GROUPS = 8
GLA_SUB = 16
GLA_GROUP = 128
GLA_ROWS = 512


def _dot(a, b):
    return jnp.dot(a, b, preferred_element_type=F32)


def _dot_nt(a, b):
    return lax.dot_general(a, b, (((1,), (1,)), ((), ())), preferred_element_type=F32)


def _silu(x):
    return x * jax.nn.sigmoid(x)


def _mod_body(c_ref, w_ref, b_ref, o_ref):
    s = _silu(c_ref[...])
    o_ref[...] = jnp.dot(s, w_ref[...], precision=HIGHEST,
                         preferred_element_type=F32) + b_ref[...]


def _modulation(c, w_mod, b_mod):
    B, D = c.shape
    N = w_mod.shape[1]
    tn = D
    return pl.pallas_call(
        _mod_body,
        grid=(N // tn,),
        in_specs=[pl.BlockSpec((B, D), lambda j: (0, 0)),
                  pl.BlockSpec((D, tn), lambda j: (0, j)),
                  pl.BlockSpec((1, tn), lambda j: (0, j))],
        out_specs=pl.BlockSpec((B, tn), lambda j: (0, j)),
        out_shape=jax.ShapeDtypeStruct((B, N), F32),
    )(c, w_mod, b_mod)


def _inproj_body(x_ref, sc_ref, sh_ref, g_ref, wa_ref, wm_ref, wg_ref, wt_ref,
                 qt_o, k_o, qit_o, misc_o, ki_o, gq_o, gk_o, gv_o, go_o, vt_o, gvt_o, misct_o):
    x = x_ref[0]
    ms = jnp.mean(x * x, axis=-1, keepdims=True)
    y = (x * lax.rsqrt(ms + NORM_EPS)) * g_ref[...]
    h = y * (1.0 + sc_ref[0]) + sh_ref[0]
    hb = h.astype(BF16)
    w = ATT_WIDTH
    k_o[0] = _dot(hb, wa_ref[...]).astype(BF16)
    pm = _dot(hb, wm_ref[...])
    misc_o[0] = pm
    ki_o[0] = pm[:, 0:IDX_DIM].astype(BF16)
    pg = _dot(hb, wg_ref[...])
    gq_o[0] = pg[:, 0:GLA_KW]
    gk_o[0] = pg[:, GLA_KW:2 * GLA_KW]
    gv_o[0] = pg[:, 2 * GLA_KW:2 * GLA_KW + GLA_WIDTH]
    go_o[0] = pg[:, 2 * GLA_KW + GLA_WIDTH:]
    pt = _dot_nt(wt_ref[...], hb)
    qt_o[0] = (pt[0:w, :] * (ATT_HEAD_DIM ** -0.5 * LOG2_E)).astype(BF16)
    qit_o[0] = pt[w:2 * w, :].astype(BF16)
    vt_o[0] = pt[2 * w:3 * w, :].astype(BF16)
    gvt_o[0] = pt[3 * w:3 * w + GLA_WIDTH, :]
    misct_o[0] = pt[3 * w + GLA_WIDTH:, :]


def _input_projection(x, sc1, sh1, norm1_g, w_in):
    B, L, D = x.shape
    tm = min(ROWS_PROJ, L)
    o = [0]
    for s in (ATT_WIDTH, ATT_WIDTH, ATT_WIDTH, IDX_HEADS * IDX_DIM, IDX_DIM, IDX_HEADS,
              GLA_KW, GLA_KW, GLA_WIDTH, GLA_WIDTH, GLA_GATE_RANK):
        o.append(o[-1] + s)
    col = lambda i: w_in[:, o[i]:o[i + 1]]
    wa = col(1).astype(BF16)
    pad = jnp.zeros((D, MISC_W - IDX_DIM - IDX_HEADS - GLA_GATE_RANK), w_in.dtype)
    wm = jnp.concatenate([col(4), col(5), col(10), pad], axis=1).astype(BF16)
    wg = jnp.concatenate([col(6), col(7), col(8), col(9)], axis=1).astype(BF16)
    wt = jnp.concatenate([col(0), col(3), col(2), col(8), wm], axis=1).T.astype(BF16)

    row = lambda w, dt: jax.ShapeDtypeStruct((B, L, w), dt)
    feat = lambda w, dt: jax.ShapeDtypeStruct((B, w, L), dt)
    row_spec = lambda w: pl.BlockSpec((1, tm, w), lambda b, i: (b, i, 0))
    feat_spec = lambda w: pl.BlockSpec((1, w, tm), lambda b, i: (b, 0, i))
    full = lambda a: pl.BlockSpec(a.shape, lambda b, i: (0,) * a.ndim,
                                  pipeline_mode=pl.Buffered(1))
    vec = pl.BlockSpec((1, 1, D), lambda b, i: (b, 0, 0))
    return pl.pallas_call(
        _inproj_body,
        grid=(B, L // tm),
        in_specs=[row_spec(D), vec, vec, full(norm1_g), full(wa), full(wm), full(wg), full(wt)],
        out_specs=[feat_spec(ATT_WIDTH), row_spec(ATT_WIDTH),
                   feat_spec(IDX_HEADS * IDX_DIM), row_spec(MISC_W), row_spec(IDX_DIM),
                   row_spec(GLA_KW), row_spec(GLA_KW), row_spec(GLA_WIDTH), row_spec(GLA_WIDTH),
                   feat_spec(ATT_WIDTH), feat_spec(GLA_WIDTH), feat_spec(MISC_W)],
        out_shape=[feat(ATT_WIDTH, BF16), row(ATT_WIDTH, BF16),
                   feat(IDX_HEADS * IDX_DIM, BF16), row(MISC_W, F32), row(IDX_DIM, BF16),
                   row(GLA_KW, F32), row(GLA_KW, F32), row(GLA_WIDTH, F32), row(GLA_WIDTH, F32),
                   feat(ATT_WIDTH, BF16), feat(GLA_WIDTH, F32), feat(MISC_W, F32)],
        compiler_params=pltpu.CompilerParams(
            dimension_semantics=("parallel", "parallel"), vmem_limit_bytes=VMEM_LIMIT),
    )(x, sc1, sh1, norm1_g, wa, wm, wg, wt)


def _bit_transpose(words):
    x = list(words)
    for shift, mask in ((16, 0x0000FFFF), (8, 0x00FF00FF), (4, 0x0F0F0F0F),
                        (2, 0x33333333), (1, 0x55555555)):
        for lo in range(WORD_BITS):
            if lo & shift == 0:
                a, b = x[lo], x[lo + shift]
                t = (a ^ lax.shift_right_logical(b, shift)) & mask
                x[lo] = a ^ t
                x[lo + shift] = b ^ lax.shift_left(t, shift)
    return x


def _dsa_body(qit_ref, misct_ref, ki_ref, qt_ref, k_ref, vt_ref, o_ref,
              keys_ref, planes_ref, cand_ref, dta_ref, dtb_ref, qka_ref, qkb_ref,
              mxa_ref, mxb_ref, m_ref, acc_ref, *, n_sel, seq_len):
    i = pl.program_id(1)
    tq, ck = Q_TILE, K_CHUNK
    n_full = (i * tq) // ck
    n_chunks = n_full + 1
    idx_scale = (IDX_HEADS ** -0.5) * (IDX_DIM ** -0.5)

    s_iota = lax.broadcasted_iota(I32, (ck, tq), 0)
    t_pos = i * tq + lax.broadcasted_iota(I32, (ck, tq), 1)
    w = misct_ref[0][W_IDX_OFF:W_IDX_OFF + IDX_HEADS, :] * idx_scale
    qit_heads = [qit_ref[0, h * IDX_DIM:(h + 1) * IDX_DIM, :] for h in range(IDX_HEADS)]
    w_rows = [w[h:h + 1, :] for h in range(IDX_HEADS)]

    half = GROUP_KEYS
    s_half = lax.broadcasted_iota(I32, (half, tq), 0)
    t_half = i * tq + lax.broadcasted_iota(I32, (half, tq), 1)

    def dots_to(buf, u):
        kc = ki_ref[0, pl.ds(pl.multiple_of(u * half, half), half), :]
        for h in range(IDX_HEADS):
            buf[h] = _dot(kc, qit_heads[h])

    def keys_from(buf, c, second, diagonal):
        u = 2 * c + second
        score = jnp.zeros((half, tq), F32)
        for h in range(IDX_HEADS):
            score = score + jnp.maximum(buf[h], 0.0) * w_rows[h]
        bits = lax.bitcast_convert_type(score, I32)
        key = jnp.where(bits < 0, INT_MIN - bits, bits)
        if diagonal:
            key = jnp.where(u * half + s_half <= t_half, key, KEY_MASKED)
        keys_ref[c, second * half:(second + 1) * half, :] = key

    dots_to(dta_ref, 0)

    def score_pair(c, carry):
        dots_to(dtb_ref, 2 * c + 1)
        keys_from(dta_ref, c, 0, diagonal=False)
        dots_to(dta_ref, 2 * c + 2)
        keys_from(dtb_ref, c, 1, diagonal=False)
        return carry

    lax.fori_loop(0, n_full, score_pair, 0)
    n_halves = GROUPS_PER_CHUNK * n_chunks

    @pl.when(i % 2 == 0)
    def _():
        keys_from(dta_ref, n_full, 0, diagonal=True)
        keys_ref[n_full, half:, :] = jnp.full((half, tq), KEY_MASKED, I32)

    @pl.when(i % 2 == 1)
    def _():
        dots_to(dtb_ref, i)
        keys_from(dta_ref, n_full, 0, diagonal=False)
        keys_from(dtb_ref, n_full, 1, diagonal=True)

    def slice_chunk(c, carry):
        for second in range(GROUPS_PER_CHUNK):
            u = GROUPS_PER_CHUNK * c + second
            for lb in range(tq // LANES):
                lanes = slice(lb * LANES, (lb + 1) * LANES)
                rows = [keys_ref[c, second * half + m * SUBLANES:
                                 second * half + (m + 1) * SUBLANES, lanes] ^ INT_MIN
                        for m in range(WORD_BITS)]
                for p, plane in enumerate(_bit_transpose(rows)):
                    planes_ref[p, u, :, lanes] = plane
            cand_ref[u] = jnp.full((SUBLANES, tq), -1, I32)
        return carry

    lax.fori_loop(0, n_chunks, slice_chunk, 0)

    sg = SEARCH_GROUPS
    n_steps = (n_halves + sg - 1) // sg

    def pad_group(g, carry):
        planes_ref[:, g] = jnp.zeros((WORD_BITS, SUBLANES, tq), I32)
        cand_ref[g] = jnp.zeros((SUBLANES, tq), I32)
        return carry

    lax.fori_loop(n_halves, sg * n_steps, pad_group, 0)

    def count_over_groups(words_of):
        def body(step, acc):
            words = words_of(pl.multiple_of(step * sg, sg))
            return acc + jnp.sum(lax.population_count(words).astype(F32), axis=0)
        acc = lax.fori_loop(0, n_steps, body, jnp.zeros((SUBLANES, tq), F32))
        return jnp.sum(acc, axis=0, keepdims=True)

    def decide(cnt, rank, kth_u):
        take = cnt >= rank
        return (jnp.where(take, 0, -1), jnp.where(take, rank, rank - cnt),
                lax.shift_left(kth_u, 1) | jnp.where(take, 1, 0))

    carry = decide(
        count_over_groups(lambda g0: cand_ref[pl.ds(g0, sg)] & planes_ref[0, pl.ds(g0, sg)]),
        jnp.full((1, tq), float(n_sel), F32), jnp.zeros((1, tq), I32))

    def search_pass(p, carry):
        flip, rank, kth_u = carry

        def candidates_with_bit(g0):
            gs = pl.ds(g0, sg)
            cand = cand_ref[gs] & (planes_ref[p - 1, gs] ^ flip)
            cand_ref[gs] = cand
            return cand & planes_ref[p, gs]

        return decide(count_over_groups(candidates_with_bit), rank, kth_u)

    flip, rank, kth_u = lax.fori_loop(1, WORD_BITS, search_pass, carry)

    def equal_to_threshold(g0):
        gs = pl.ds(g0, sg)
        cand = cand_ref[gs] & (planes_ref[WORD_BITS - 1, gs] ^ flip)
        cand_ref[gs] = cand
        return cand

    n_equal = count_over_groups(equal_to_threshold)
    kth = kth_u ^ INT_MIN
    has_tie = (n_equal > rank) & (kth > KEY_MASKED)
    kth = jnp.maximum(kth, KEY_MASKED + 1)

    @pl.when(jnp.max(jnp.where(has_tie, 1.0, 0.0)) > 0.0)
    def _():
        grp = lax.broadcasted_iota(I32, (sg, SUBLANES, tq), 0)
        sub = lax.broadcasted_iota(I32, (sg, SUBLANES, tq), 1)
        sub_bits = SUBLANES.bit_length() - 1

        def ties_before(bound):
            def words_of(g0):
                first = (g0 + grp) * GROUP_KEYS + sub
                n_low = lax.shift_right_arithmetic(bound - first + (SUBLANES - 1), sub_bits)
                n_low = jnp.clip(n_low, 0, WORD_BITS)
                mask = jnp.where(n_low > 0,
                                 lax.shift_left(jnp.int32(-1), WORD_BITS - jnp.maximum(n_low, 1)), 0)
                return cand_ref[pl.ds(g0, sg)] & mask
            return count_over_groups(words_of)

        n_bits = (seq_len - 1).bit_length()

        def idx_step(it, lo):
            cand = lo + lax.shift_left(jnp.int32(1), n_bits - 1 - it)
            return jnp.where(ties_before(cand) < rank, cand, lo)

        last = lax.fori_loop(0, n_bits, idx_step, jnp.zeros((1, tq), I32))
        last_kept = jnp.where(has_tie, last, seq_len)

        def demote(c, carry):
            kk = keys_ref[c]
            s_pos = c * ck + s_iota
            drop = (kk == kth) & (s_pos > last_kept)
            keys_ref[c] = jnp.where(drop, kk - 1, kk)
            return carry

        lax.fori_loop(0, n_chunks, demote, 0)

    hd = ATT_HEAD_DIM
    qt_heads = [qt_ref[0, h * hd:(h + 1) * hd, :] for h in range(ATT_HEADS)]
    ones_rows = jnp.ones((PACKED_ROWS, ck), BF16)

    heads = [slice(h * hd, (h + 1) * hd) for h in range(ATT_HEADS)]

    def qk_to(slot, c):
        buf, cmax = slot
        kc = k_ref[0, pl.ds(pl.multiple_of(c * ck, ck), ck), :]
        bias = jnp.where(keys_ref[c] >= kth, 0.0, MASKED_LOGIT)
        for h, sl in enumerate(heads):
            logits = _dot(kc[:, sl], qt_heads[h]) + bias
            buf[h] = logits
            cmax[h] = jnp.max(logits, axis=0, keepdims=True)

    def softmax_pv(c, slot):
        buf, cmax = slot
        start = pl.multiple_of(c * ck, ck)
        ms = [m_ref[h] for h in range(ATT_HEADS)]
        new_ms = [jnp.maximum(ms[h], cmax[h]) for h in range(ATT_HEADS)]
        ps = [jnp.exp2(buf[h] - new_ms[h]).astype(BF16) for h in range(ATT_HEADS)]
        for h, sl in enumerate(heads):
            vt_ext = jnp.concatenate([vt_ref[0, sl, pl.ds(start, ck)], ones_rows], axis=0)
            acc_ref[h] = jnp.exp2(ms[h] - new_ms[h]) * acc_ref[h] + _dot(vt_ext, ps[h])
            m_ref[h] = new_ms[h]

    slot_a, slot_b = (qka_ref, mxa_ref), (qkb_ref, mxb_ref)
    m_ref[...] = jnp.full_like(m_ref, MASKED_LOGIT)
    acc_ref[...] = jnp.zeros_like(acc_ref)
    qk_to(slot_a, 0)

    def attend_pair(j, carry):
        c = 2 * j
        qk_to(slot_b, c + 1)
        softmax_pv(c, slot_a)
        qk_to(slot_a, c + 2)
        softmax_pv(c + 1, slot_b)
        return carry

    lax.fori_loop(0, n_full // 2, attend_pair, 0)

    @pl.when(n_full % 2 == 0)
    def _():
        softmax_pv(n_full, slot_a)

    @pl.when(n_full % 2 == 1)
    def _():
        qk_to(slot_b, n_full)
        softmax_pv(n_full - 1, slot_a)
        softmax_pv(n_full, slot_b)

    for h in range(ATT_HEADS):
        o_ref[0, h * hd:(h + 1) * hd, :] = acc_ref[h, :hd, :] / acc_ref[h, hd:hd + 1, :]


def _dsa_attention(qt, k, vt, qit, ki, misct):
    B, L, _ = k.shape
    tq = Q_TILE
    assert tq == GROUP_KEYS and K_CHUNK == 2 * tq and L % K_CHUNK == 0
    n_sel = min(TOPK_MAX, L // 4)
    assert n_sel <= K_CHUNK
    body = functools.partial(_dsa_body, n_sel=n_sel, seq_len=L)
    feat_spec = lambda w: pl.BlockSpec((1, w, tq), lambda b, i: (b, 0, i))
    seq_spec = lambda a: pl.BlockSpec((1,) + a.shape[1:], lambda b, i: (b, 0, 0),
                                      pipeline_mode=pl.Buffered(1))
    n_groups = pl.cdiv(GROUPS_PER_CHUNK * (L // K_CHUNK), SEARCH_GROUPS) * SEARCH_GROUPS
    return pl.pallas_call(
        body,
        grid=(B, L // tq),
        in_specs=[feat_spec(IDX_HEADS * IDX_DIM), feat_spec(MISC_W), seq_spec(ki),
                  feat_spec(ATT_WIDTH), seq_spec(k), seq_spec(vt)],
        out_specs=feat_spec(ATT_WIDTH),
        out_shape=jax.ShapeDtypeStruct((B, ATT_WIDTH, L), F32),
        scratch_shapes=[
            pltpu.VMEM((L // K_CHUNK, K_CHUNK, tq), I32),
            pltpu.VMEM((WORD_BITS, n_groups, SUBLANES, tq), I32),
            pltpu.VMEM((n_groups, SUBLANES, tq), I32),
            pltpu.VMEM((IDX_HEADS, GROUP_KEYS, tq), F32), pltpu.VMEM((IDX_HEADS, GROUP_KEYS, tq), F32),
            pltpu.VMEM((ATT_HEADS, K_CHUNK, tq), F32), pltpu.VMEM((ATT_HEADS, K_CHUNK, tq), F32),
            pltpu.VMEM((ATT_HEADS, 1, tq), F32), pltpu.VMEM((ATT_HEADS, 1, tq), F32),
            pltpu.VMEM((ATT_HEADS, 1, tq), F32),
            pltpu.VMEM((ATT_HEADS, ATT_HEAD_DIM + PACKED_ROWS, tq), F32)],
        compiler_params=pltpu.CompilerParams(
            dimension_semantics=("parallel", "arbitrary"), vmem_limit_bytes=VMEM_LIMIT),
    )(qit, misct, ki, qt, k, vt)


def _gla_body(gq_ref, gk_ref, gvt_ref, gv_ref, misc_ref, w2_ref, b2_ref, o_ref, st_ref):
    n, kw, vw = GLA_SUB, GLA_KW, GLA_WIDTH

    @pl.when(pl.program_id(1) == 0)
    def _():
        st_ref[...] = jnp.zeros_like(st_ref)

    rows = gq_ref.shape[1]
    grp = GLA_GROUP
    hv = lax.broadcasted_iota(I32, (vw, kw), 0) // GLA_DV
    hd = lax.broadcasted_iota(I32, (vw, kw), 1) // GLA_DK
    st_mask = hv == hd
    ed = lax.broadcasted_iota(I32, (kw, vw), 0) // GLA_DK
    ev = lax.broadcasted_iota(I32, (kw, vw), 1) // GLA_DV
    expand = jnp.where(ed == ev, 1.0, 0.0).astype(BF16)
    row_i = lax.broadcasted_iota(I32, (n, kw), 0)
    in_blk = lax.broadcasted_iota(I32, (grp, kw), 0) % n
    scale = GLA_DK ** -0.5

    def group(gi, carry):
        g0 = pl.multiple_of(gi * grp, grp)
        g_lr = misc_ref[0, pl.ds(g0, grp), :][:, G_LR_OFF:G_LR_OFF + GLA_GATE_RANK]
        z = jnp.dot(g_lr, w2_ref[...], precision=HIGHEST,
                    preferred_element_type=F32) + b2_ref[...]
        log_a = (jnp.minimum(z, 0.0) - jnp.log1p(jnp.exp(-jnp.abs(z)))) * (1.0 / GLA_TAU)
        b_all = log_a
        shift = 1
        while shift < n:
            b_all = b_all + jnp.where(in_blk >= shift, pltpu.roll(b_all, shift, axis=0), 0.0)
            shift *= 2
        q_all = gq_ref[0, pl.ds(g0, grp), :] * scale
        k_all = gk_ref[0, pl.ds(g0, grp), :]
        v_all = gv_ref[0, pl.ds(g0, grp), :]
        vt_all = gvt_ref[0, :, pl.ds(g0, grp)]
        for blk in range(grp // n):
            sl = slice(blk * n, (blk + 1) * n)
            _gla_block(q_all[sl], k_all[sl], v_all[sl], vt_all[:, sl], b_all[sl],
                       o_ref.at[0, pl.ds(g0 + blk * n, n), :], st_ref,
                       st_mask, expand, row_i)
        return carry

    lax.fori_loop(0, rows // grp, group, 0)


def _gla_block(qb, kb, vb, vtb, b, o_blk, st_ref, st_mask, expand, row_i):
    n, vw = GLA_SUB, GLA_WIDTH
    st = st_ref[...]
    o_inter = _dot_nt((qb * jnp.exp(b)).astype(BF16), st.astype(BF16))
    terms = []
    for j in range(n):
        diff = jnp.where(row_i >= j, b - b[j:j + 1, :], -jnp.inf)
        terms.append(qb * kb[j:j + 1, :] * jnp.exp(diff))
    t_all = jnp.concatenate(terms, axis=0).astype(BF16)
    g_all = _dot(t_all, expand)
    o_intra = jnp.zeros((n, vw), F32)
    for j in range(n):
        o_intra = o_intra + g_all[j * n:(j + 1) * n, :] * vb[j:j + 1, :]
    o_blk[...] = o_inter + o_intra
    b_last = b[n - 1:n, :]
    k_dec = (kb * jnp.exp(b_last - b)).astype(BF16)
    upd = _dot(vtb.astype(BF16), k_dec)
    st_ref[...] = st * jnp.exp(b_last) + jnp.where(st_mask, upd, 0.0)


def _gla(gq, gk, gvt, gv, misc, w_gate2, b_gate2):
    B, L, _ = gq.shape
    rows = min(GLA_ROWS, L)
    row_spec = lambda w: pl.BlockSpec((1, rows, w), lambda b, i: (b, i, 0))
    full = lambda a: pl.BlockSpec(a.shape, lambda b, i: (0,) * a.ndim)
    return pl.pallas_call(
        _gla_body,
        grid=(B, L // rows),
        in_specs=[row_spec(GLA_KW), row_spec(GLA_KW),
                  pl.BlockSpec((1, GLA_WIDTH, rows), lambda b, i: (b, 0, i)),
                  row_spec(GLA_WIDTH), row_spec(MISC_W), full(w_gate2), full(b_gate2)],
        out_specs=row_spec(GLA_WIDTH),
        out_shape=jax.ShapeDtypeStruct((B, L, GLA_WIDTH), F32),
        scratch_shapes=[pltpu.VMEM((GLA_WIDTH, GLA_KW), F32)],
        compiler_params=pltpu.CompilerParams(
            dimension_semantics=("parallel", "arbitrary"), vmem_limit_bytes=VMEM_LIMIT),
    )(gq, gk, gvt, gv, misc, w_gate2, b_gate2)


def _rms(x, g):
    ms = jnp.mean(x * x, axis=-1, keepdims=True)
    return (x * lax.rsqrt(ms + NORM_EPS)) * g


def _mix_ffn_body(x_ref, attt_ref, gla_ref, go_ref, g1_ref, sc_ref, sh_ref, g2_ref,
                  ag_ref, gg_ref, ng_ref, fg_ref, wo_ref, wgu_ref, wd_ref, o_ref, *, d_ff, final):
    a = _rms(attt_ref[0].T, ag_ref[...])
    gl = gla_ref[0]
    parts = []
    for h in range(GLA_HEADS):
        sl = slice(h * GLA_DV, (h + 1) * GLA_DV)
        parts.append(_rms(gl[:, sl], gg_ref[...]))
    g = jnp.concatenate(parts, axis=1) * _silu(go_ref[0])
    mix = (_dot(a.astype(BF16), wo_ref[0:ATT_WIDTH, :])
           + _dot(g.astype(BF16), wo_ref[ATT_WIDTH:, :]))
    x = x_ref[0] + g1_ref[0] * mix
    h = _rms(x, ng_ref[...]) * (1.0 + sc_ref[0]) + sh_ref[0]
    gu = _dot(h.astype(BF16), wgu_ref[...])
    act = _silu(gu[:, :d_ff]) * gu[:, d_ff:]
    y = x + g2_ref[0] * _dot(act.astype(BF16), wd_ref[...])
    o_ref[0] = _rms(y, fg_ref[...]) if final else y


def _mix_ffn(x, att_t, gla, go, g1, sc2, sh2, g2, att_out_g, gla_out_g, norm2_g, final_g,
             w_out, w_gate_up, w_down, final):
    B, L, D = x.shape
    d_ff = w_down.shape[0]
    tm = min(ROWS_FFN, L)
    row_spec = lambda w: pl.BlockSpec((1, tm, w), lambda b, i: (b, i, 0))
    full = lambda a: pl.BlockSpec(a.shape, lambda b, i: (0,) * a.ndim,
                                  pipeline_mode=pl.Buffered(1))
    vec = pl.BlockSpec((1, 1, D), lambda b, i: (b, 0, 0))
    wo = w_out.astype(BF16)
    wgu = w_gate_up.astype(BF16)
    wd = w_down.astype(BF16)
    return pl.pallas_call(
        functools.partial(_mix_ffn_body, d_ff=d_ff, final=final),
        grid=(B, L // tm),
        in_specs=[row_spec(D), pl.BlockSpec((1, ATT_WIDTH, tm), lambda b, i: (b, 0, i)),
                  row_spec(GLA_WIDTH), row_spec(GLA_WIDTH), vec, vec, vec, vec,
                  full(att_out_g), full(gla_out_g), full(norm2_g), full(final_g),
                  full(wo), full(wgu), full(wd)],
        out_specs=row_spec(D),
        out_shape=jax.ShapeDtypeStruct((B, L, D), F32),
        compiler_params=pltpu.CompilerParams(
            dimension_semantics=("parallel", "parallel"), vmem_limit_bytes=VMEM_LIMIT),
    )(x, att_t, gla, go, g1, sc2, sh2, g2, att_out_g, gla_out_g, norm2_g, final_g, wo, wgu, wd)


def kernel(x, c, w_mod, b_mod, norm1_g, w_in, w_gate2, b_gate2, att_out_g, gla_out_g,
           w_out, norm2_g, w_gate_up, w_down, final_g):
    B, L, D = x.shape
    depth = w_mod.shape[0]
    for l in range(depth):
        mod = _modulation(c, w_mod[l], b_mod[l][None, :])
        sh1, sc1, g1, sh2, sc2, g2 = [m[:, None, :] for m in jnp.split(mod, N_MOD, axis=-1)]
        qt, k, qit, misc, ki, gq, gk, gv, go, vt, gvt, misct = _input_projection(
            x, sc1, sh1, norm1_g[l][None, :], w_in[l])
        att_t = _dsa_attention(qt, k, vt, qit, ki, misct)
        gla = _gla(gq, gk, gvt, gv, misc, w_gate2[l], b_gate2[l][None, :])
        x = _mix_ffn(x, att_t, gla, go, g1, sc2, sh2, g2, att_out_g[l][None, :],
                     gla_out_g[l][None, :], norm2_g[l][None, :], final_g[None, :],
                     w_out[l], w_gate_up[l], w_down[l], final=(l == depth - 1))
    return x
```

```python
import functools

import jax
import jax.numpy as jnp
from jax import lax
from jax.experimental import pallas as pl
from jax.experimental.pallas import tpu as pltpu

F32 = jnp.float32
BF16 = jnp.bfloat16
I32 = jnp.int32
HIGHEST = lax.Precision.HIGHEST

NORM_EPS = 1e-6
ATT_HEADS = 4
ATT_HEAD_DIM = 128
ATT_WIDTH = ATT_HEADS * ATT_HEAD_DIM
IDX_HEADS = 8
IDX_DIM = 64
TOPK_MAX = 256
GLA_HEADS = 4
GLA_DK = 64
GLA_DV = 128
GLA_KW = GLA_HEADS * GLA_DK
GLA_WIDTH = GLA_HEADS * GLA_DV
GLA_GATE_RANK = 16
GLA_TAU = 16.0
N_MOD = 6

LANES = 128
SUBLANES = 8
PACKED_ROWS = 16
MISC_W = LANES
W_IDX_OFF = IDX_DIM
G_LR_OFF = IDX_DIM + IDX_HEADS

INT_MIN = -2 ** 31
KEY_MASKED = INT_MIN
MASKED_LOGIT = -1e30
LOG2_E = 1.4426950408889634
VMEM_LIMIT = 56 * 1024 * 1024

ROWS_PROJ = 512
ROWS_FFN = 256
Q_TILE = 256
K_CHUNK = 512
WORD_BITS = 32
GROUP_KEYS = WORD_BITS * SUBLANES
GROUPS_PER_CHUNK = K_CHUNK // GROUP_KEYS
SEARCH_GROUPS = 8
GLA_SUB = 16
GLA_GROUP = 128
GLA_ROWS = 512


def _dot(a, b):
    return jnp.dot(a, b, preferred_element_type=F32)


def _dot_nt(a, b):
    return lax.dot_general(a, b, (((1,), (1,)), ((), ())), preferred_element_type=F32)


def _silu(x):
    return x * jax.nn.sigmoid(x)


def _mod_body(c_ref, w_ref, b_ref, o_ref):
    s = _silu(c_ref[...])
    o_ref[...] = jnp.dot(s, w_ref[...], precision=HIGHEST,
                         preferred_element_type=F32) + b_ref[...]


def _modulation(c, w_mod, b_mod):
    B, D = c.shape
    N = w_mod.shape[1]
    tn = D
    return pl.pallas_call(
        _mod_body,
        grid=(N // tn,),
        in_specs=[pl.BlockSpec((B, D), lambda j: (0, 0)),
                  pl.BlockSpec((D, tn), lambda j: (0, j)),
                  pl.BlockSpec((1, tn), lambda j: (0, j))],
        out_specs=pl.BlockSpec((B, tn), lambda j: (0, j)),
        out_shape=jax.ShapeDtypeStruct((B, N), F32),
    )(c, w_mod, b_mod)


def _inproj_body(x_ref, sc_ref, sh_ref, g_ref, wa_ref, wm_ref, wg_ref, wt_ref,
                 qt_o, k_o, qit_o, misc_o, ki_o, gq_o, gk_o, gv_o, go_o, vt_o, gvt_o, misct_o):
    x = x_ref[0]
    ms = jnp.mean(x * x, axis=-1, keepdims=True)
    y = (x * lax.rsqrt(ms + NORM_EPS)) * g_ref[...]
    h = y * (1.0 + sc_ref[0]) + sh_ref[0]
    hb = h.astype(BF16)
    w = ATT_WIDTH
    k_o[0] = _dot(hb, wa_ref[...]).astype(BF16)
    pm = _dot(hb, wm_ref[...])
    misc_o[0] = pm
    ki_o[0] = pm[:, 0:IDX_DIM].astype(BF16)
    pg = _dot(hb, wg_ref[...])
    gq_o[0] = pg[:, 0:GLA_KW]
    gk_o[0] = pg[:, GLA_KW:2 * GLA_KW]
    gv_o[0] = pg[:, 2 * GLA_KW:2 * GLA_KW + GLA_WIDTH]
    go_o[0] = pg[:, 2 * GLA_KW + GLA_WIDTH:]
    pt = _dot_nt(wt_ref[...], hb)
    qt_o[0] = (pt[0:w, :] * (ATT_HEAD_DIM ** -0.5 * LOG2_E)).astype(BF16)
    qit_o[0] = pt[w:2 * w, :].astype(BF16)
    vt_o[0] = pt[2 * w:3 * w, :].astype(BF16)
    gvt_o[0] = pt[3 * w:3 * w + GLA_WIDTH, :]
    misct_o[0] = pt[3 * w + GLA_WIDTH:, :]


def _input_projection(x, sc1, sh1, norm1_g, w_in):
    B, L, D = x.shape
    tm = min(ROWS_PROJ, L)
    o = [0]
    for s in (ATT_WIDTH, ATT_WIDTH, ATT_WIDTH, IDX_HEADS * IDX_DIM, IDX_DIM, IDX_HEADS,
              GLA_KW, GLA_KW, GLA_WIDTH, GLA_WIDTH, GLA_GATE_RANK):
        o.append(o[-1] + s)
    col = lambda i: w_in[:, o[i]:o[i + 1]]
    wa = col(1).astype(BF16)
    pad = jnp.zeros((D, MISC_W - IDX_DIM - IDX_HEADS - GLA_GATE_RANK), w_in.dtype)
    wm = jnp.concatenate([col(4), col(5), col(10), pad], axis=1).astype(BF16)
    wg = jnp.concatenate([col(6), col(7), col(8), col(9)], axis=1).astype(BF16)
    wt = jnp.concatenate([col(0), col(3), col(2), col(8), wm], axis=1).T.astype(BF16)

    row = lambda w, dt: jax.ShapeDtypeStruct((B, L, w), dt)
    feat = lambda w, dt: jax.ShapeDtypeStruct((B, w, L), dt)
    row_spec = lambda w: pl.BlockSpec((1, tm, w), lambda b, i: (b, i, 0))
    feat_spec = lambda w: pl.BlockSpec((1, w, tm), lambda b, i: (b, 0, i))
    full = lambda a: pl.BlockSpec(a.shape, lambda b, i: (0,) * a.ndim,
                                  pipeline_mode=pl.Buffered(1))
    vec = pl.BlockSpec((1, 1, D), lambda b, i: (b, 0, 0))
    return pl.pallas_call(
        _inproj_body,
        grid=(B, L // tm),
        in_specs=[row_spec(D), vec, vec, full(norm1_g), full(wa), full(wm), full(wg), full(wt)],
        out_specs=[feat_spec(ATT_WIDTH), row_spec(ATT_WIDTH),
                   feat_spec(IDX_HEADS * IDX_DIM), row_spec(MISC_W), row_spec(IDX_DIM),
                   row_spec(GLA_KW), row_spec(GLA_KW), row_spec(GLA_WIDTH), row_spec(GLA_WIDTH),
                   feat_spec(ATT_WIDTH), feat_spec(GLA_WIDTH), feat_spec(MISC_W)],
        out_shape=[feat(ATT_WIDTH, BF16), row(ATT_WIDTH, BF16),
                   feat(IDX_HEADS * IDX_DIM, BF16), row(MISC_W, F32), row(IDX_DIM, BF16),
                   row(GLA_KW, F32), row(GLA_KW, F32), row(GLA_WIDTH, F32), row(GLA_WIDTH, F32),
                   feat(ATT_WIDTH, BF16), feat(GLA_WIDTH, F32), feat(MISC_W, F32)],
        compiler_params=pltpu.CompilerParams(
            dimension_semantics=("parallel", "parallel"), vmem_limit_bytes=VMEM_LIMIT),
    )(x, sc1, sh1, norm1_g, wa, wm, wg, wt)


def _bit_transpose(words):
    x = list(words)
    for shift, mask in ((16, 0x0000FFFF), (8, 0x00FF00FF), (4, 0x0F0F0F0F),
                        (2, 0x33333333), (1, 0x55555555)):
        for lo in range(WORD_BITS):
            if lo & shift == 0:
                a, b = x[lo], x[lo + shift]
                t = (a ^ lax.shift_right_logical(b, shift)) & mask
                x[lo] = a ^ t
                x[lo + shift] = b ^ lax.shift_left(t, shift)
    return x


def _dsa_body(qit_ref, misct_ref, ki_ref, qt_ref, k_ref, vt_ref, o_ref,
              keys_ref, planes_ref, cand_ref, dta_ref, dtb_ref, qka_ref, qkb_ref,
              mxa_ref, mxb_ref, m_ref, acc_ref, *, n_sel, seq_len):
    i = pl.program_id(1)
    tq, ck = Q_TILE, K_CHUNK
    n_full = (i * tq) // ck
    n_chunks = n_full + 1
    idx_scale = (IDX_HEADS ** -0.5) * (IDX_DIM ** -0.5)

    s_iota = lax.broadcasted_iota(I32, (ck, tq), 0)
    t_pos = i * tq + lax.broadcasted_iota(I32, (ck, tq), 1)
    w = misct_ref[0][W_IDX_OFF:W_IDX_OFF + IDX_HEADS, :] * idx_scale
    qit_heads = [qit_ref[0, h * IDX_DIM:(h + 1) * IDX_DIM, :] for h in range(IDX_HEADS)]
    w_rows = [w[h:h + 1, :] for h in range(IDX_HEADS)]

    half = GROUP_KEYS
    s_half = lax.broadcasted_iota(I32, (half, tq), 0)
    t_half = i * tq + lax.broadcasted_iota(I32, (half, tq), 1)

    def dots_to(buf, u):
        kc = ki_ref[0, pl.ds(pl.multiple_of(u * half, half), half), :]
        for h in range(IDX_HEADS):
            buf[h] = _dot(kc, qit_heads[h])

    def keys_from(buf, c, second, diagonal):
        u = 2 * c + second
        score = jnp.zeros((half, tq), F32)
        for h in range(IDX_HEADS):
            score = score + jnp.maximum(buf[h], 0.0) * w_rows[h]
        bits = lax.bitcast_convert_type(score, I32)
        key = jnp.where(bits < 0, INT_MIN - bits, bits)
        if diagonal:
            key = jnp.where(u * half + s_half <= t_half, key, KEY_MASKED)
        keys_ref[c, second * half:(second + 1) * half, :] = key

    dots_to(dta_ref, 0)

    def score_pair(c, carry):
        dots_to(dtb_ref, 2 * c + 1)
        keys_from(dta_ref, c, 0, diagonal=False)
        dots_to(dta_ref, 2 * c + 2)
        keys_from(dtb_ref, c, 1, diagonal=False)
        return carry

    lax.fori_loop(0, n_full, score_pair, 0)
    n_halves = GROUPS_PER_CHUNK * n_chunks

    @pl.when(i % 2 == 0)
    def _():
        keys_from(dta_ref, n_full, 0, diagonal=True)
        keys_ref[n_full, half:, :] = jnp.full((half, tq), KEY_MASKED, I32)

    @pl.when(i % 2 == 1)
    def _():
        dots_to(dtb_ref, i)
        keys_from(dta_ref, n_full, 0, diagonal=False)
        keys_from(dtb_ref, n_full, 1, diagonal=True)

    def slice_chunk(c, carry):
        for second in range(GROUPS_PER_CHUNK):
            u = GROUPS_PER_CHUNK * c + second
            for lb in range(tq // LANES):
                lanes = slice(lb * LANES, (lb + 1) * LANES)
                rows = [keys_ref[c, second * half + m * SUBLANES:
                                 second * half + (m + 1) * SUBLANES, lanes] ^ INT_MIN
                        for m in range(WORD_BITS)]
                for p, plane in enumerate(_bit_transpose(rows)):
                    planes_ref[p, u, :, lanes] = plane
            cand_ref[u] = jnp.full((SUBLANES, tq), -1, I32)
        return carry

    lax.fori_loop(0, n_chunks, slice_chunk, 0)

    sg = SEARCH_GROUPS
    n_steps = (n_halves + sg - 1) // sg

    def pad_group(g, carry):
        planes_ref[:, g] = jnp.zeros((WORD_BITS, SUBLANES, tq), I32)
        cand_ref[g] = jnp.zeros((SUBLANES, tq), I32)
        return carry

    lax.fori_loop(n_halves, sg * n_steps, pad_group, 0)

    def count_over_groups(words_of):
        def body(step, acc):
            words = words_of(pl.multiple_of(step * sg, sg))
            return acc + jnp.sum(lax.population_count(words).astype(F32), axis=0)
        acc = lax.fori_loop(0, n_steps, body, jnp.zeros((SUBLANES, tq), F32))
        return jnp.sum(acc, axis=0, keepdims=True)

    def decide(cnt, rank, kth_u):
        take = cnt >= rank
        return (jnp.where(take, 0, -1), jnp.where(take, rank, rank - cnt),
                lax.shift_left(kth_u, 1) | jnp.where(take, 1, 0))

    carry = decide(
        count_over_groups(lambda g0: cand_ref[pl.ds(g0, sg)] & planes_ref[0, pl.ds(g0, sg)]),
        jnp.full((1, tq), float(n_sel), F32), jnp.zeros((1, tq), I32))

    def search_pass(p, carry):
        flip, rank, kth_u = carry

        def candidates_with_bit(g0):
            gs = pl.ds(g0, sg)
            cand = cand_ref[gs] & (planes_ref[p - 1, gs] ^ flip)
            cand_ref[gs] = cand
            return cand & planes_ref[p, gs]

        return decide(count_over_groups(candidates_with_bit), rank, kth_u)

    flip, rank, kth_u = lax.fori_loop(1, WORD_BITS, search_pass, carry)

    def equal_to_threshold(g0):
        gs = pl.ds(g0, sg)
        cand = cand_ref[gs] & (planes_ref[WORD_BITS - 1, gs] ^ flip)
        cand_ref[gs] = cand
        return cand

    n_equal = count_over_groups(equal_to_threshold)
    kth = kth_u ^ INT_MIN
    has_tie = (n_equal > rank) & (kth > KEY_MASKED)
    kth = jnp.maximum(kth, KEY_MASKED + 1)

    @pl.when(jnp.max(jnp.where(has_tie, 1.0, 0.0)) > 0.0)
    def _():
        grp = lax.broadcasted_iota(I32, (sg, SUBLANES, tq), 0)
        sub = lax.broadcasted_iota(I32, (sg, SUBLANES, tq), 1)
        sub_bits = SUBLANES.bit_length() - 1

        def ties_before(bound):
            def words_of(g0):
                first = (g0 + grp) * GROUP_KEYS + sub
                n_low = lax.shift_right_arithmetic(bound - first + (SUBLANES - 1), sub_bits)
                n_low = jnp.clip(n_low, 0, WORD_BITS)
                mask = jnp.where(n_low > 0,
                                 lax.shift_left(jnp.int32(-1), WORD_BITS - jnp.maximum(n_low, 1)), 0)
                return cand_ref[pl.ds(g0, sg)] & mask
            return count_over_groups(words_of)

        n_bits = (seq_len - 1).bit_length()

        def idx_step(it, lo):
            cand = lo + lax.shift_left(jnp.int32(1), n_bits - 1 - it)
            return jnp.where(ties_before(cand) < rank, cand, lo)

        last = lax.fori_loop(0, n_bits, idx_step, jnp.zeros((1, tq), I32))
        last_kept = jnp.where(has_tie, last, seq_len)

        def demote(c, carry):
            kk = keys_ref[c]
            s_pos = c * ck + s_iota
            drop = (kk == kth) & (s_pos > last_kept)
            keys_ref[c] = jnp.where(drop, kk - 1, kk)
            return carry

        lax.fori_loop(0, n_chunks, demote, 0)

    hd = ATT_HEAD_DIM
    qt_heads = [qt_ref[0, h * hd:(h + 1) * hd, :] for h in range(ATT_HEADS)]
    ones_rows = jnp.ones((PACKED_ROWS, ck), BF16)

    heads = [slice(h * hd, (h + 1) * hd) for h in range(ATT_HEADS)]

    def qk_to(slot, c):
        buf, cmax = slot
        kc = k_ref[0, pl.ds(pl.multiple_of(c * ck, ck), ck), :]
        bias = jnp.where(keys_ref[c] >= kth, 0.0, MASKED_LOGIT)
        for h, sl in enumerate(heads):
            logits = _dot(kc[:, sl], qt_heads[h]) + bias
            buf[h] = logits
            cmax[h] = jnp.max(logits, axis=0, keepdims=True)

    def softmax_pv(c, slot):
        buf, cmax = slot
        start = pl.multiple_of(c * ck, ck)
        ms = [m_ref[h] for h in range(ATT_HEADS)]
        new_ms = [jnp.maximum(ms[h], cmax[h]) for h in range(ATT_HEADS)]
        ps = [jnp.exp2(buf[h] - new_ms[h]).astype(BF16) for h in range(ATT_HEADS)]
        for h, sl in enumerate(heads):
            vt_ext = jnp.concatenate([vt_ref[0, sl, pl.ds(start, ck)], ones_rows], axis=0)
            acc_ref[h] = jnp.exp2(ms[h] - new_ms[h]) * acc_ref[h] + _dot(vt_ext, ps[h])
            m_ref[h] = new_ms[h]

    slot_a, slot_b = (qka_ref, mxa_ref), (qkb_ref, mxb_ref)
    m_ref[...] = jnp.full_like(m_ref, MASKED_LOGIT)
    acc_ref[...] = jnp.zeros_like(acc_ref)
    qk_to(slot_a, 0)

    def attend_pair(j, carry):
        c = 2 * j
        qk_to(slot_b, c + 1)
        softmax_pv(c, slot_a)
        qk_to(slot_a, c + 2)
        softmax_pv(c + 1, slot_b)
        return carry

    lax.fori_loop(0, n_full // 2, attend_pair, 0)

    @pl.when(n_full % 2 == 0)
    def _():
        softmax_pv(n_full, slot_a)

    @pl.when(n_full % 2 == 1)
    def _():
        qk_to(slot_b, n_full)
        softmax_pv(n_full - 1, slot_a)
        softmax_pv(n_full, slot_b)

    for h in range(ATT_HEADS):
        o_ref[0, h * hd:(h + 1) * hd, :] = acc_ref[h, :hd, :] / acc_ref[h, hd:hd + 1, :]


def _dsa_attention(qt, k, vt, qit, ki, misct):
    B, L, _ = k.shape
    tq = Q_TILE
    assert tq == GROUP_KEYS and K_CHUNK == 2 * tq and L % K_CHUNK == 0
    n_sel = min(TOPK_MAX, L // 4)
    assert n_sel <= K_CHUNK
    body = functools.partial(_dsa_body, n_sel=n_sel, seq_len=L)
    feat_spec = lambda w: pl.BlockSpec((1, w, tq), lambda b, i: (b, 0, i))
    seq_spec = lambda a: pl.BlockSpec((1,) + a.shape[1:], lambda b, i: (b, 0, 0),
                                      pipeline_mode=pl.Buffered(1))
    n_groups = pl.cdiv(GROUPS_PER_CHUNK * (L // K_CHUNK), SEARCH_GROUPS) * SEARCH_GROUPS
    return pl.pallas_call(
        body,
        grid=(B, L // tq),
        in_specs=[feat_spec(IDX_HEADS * IDX_DIM), feat_spec(MISC_W), seq_spec(ki),
                  feat_spec(ATT_WIDTH), seq_spec(k), seq_spec(vt)],
        out_specs=feat_spec(ATT_WIDTH),
        out_shape=jax.ShapeDtypeStruct((B, ATT_WIDTH, L), F32),
        scratch_shapes=[
            pltpu.VMEM((L // K_CHUNK, K_CHUNK, tq), I32),
            pltpu.VMEM((WORD_BITS, n_groups, SUBLANES, tq), I32),
            pltpu.VMEM((n_groups, SUBLANES, tq), I32),
            pltpu.VMEM((IDX_HEADS, GROUP_KEYS, tq), F32), pltpu.VMEM((IDX_HEADS, GROUP_KEYS, tq), F32),
            pltpu.VMEM((ATT_HEADS, K_CHUNK, tq), F32), pltpu.VMEM((ATT_HEADS, K_CHUNK, tq), F32),
            pltpu.VMEM((ATT_HEADS, 1, tq), F32), pltpu.VMEM((ATT_HEADS, 1, tq), F32),
            pltpu.VMEM((ATT_HEADS, 1, tq), F32),
            pltpu.VMEM((ATT_HEADS, ATT_HEAD_DIM + PACKED_ROWS, tq), F32)],
        compiler_params=pltpu.CompilerParams(
            dimension_semantics=("parallel", "arbitrary"), vmem_limit_bytes=VMEM_LIMIT),
    )(qit, misct, ki, qt, k, vt)


def _gla_body(gq_ref, gk_ref, gvt_ref, gv_ref, misc_ref, w2_ref, b2_ref, o_ref, st_ref, b_ref):
    n, kw, vw = GLA_SUB, GLA_KW, GLA_WIDTH

    @pl.when(pl.program_id(1) == 0)
    def _():
        st_ref[...] = jnp.zeros_like(st_ref)

    rows = gq_ref.shape[1]
    grp = GLA_GROUP
    lane_head = lax.broadcasted_iota(I32, (n, kw), 1) // GLA_DK
    ed = lax.broadcasted_iota(I32, (kw, vw), 0) // GLA_DK
    ev = lax.broadcasted_iota(I32, (kw, vw), 1) // GLA_DV
    expand = jnp.where(ed == ev, 1.0, 0.0).astype(BF16)
    row_i = lax.broadcasted_iota(I32, (n, kw), 0)
    in_blk = lax.broadcasted_iota(I32, (rows, kw), 0) % n
    scale = GLA_DK ** -0.5

    g_lr = misc_ref[0][:, G_LR_OFF:G_LR_OFF + GLA_GATE_RANK]
    z = jnp.dot(g_lr, w2_ref[...], precision=HIGHEST, preferred_element_type=F32) + b2_ref[...]
    log_a = (jnp.minimum(z, 0.0) - jnp.log1p(jnp.exp(-jnp.abs(z)))) * (1.0 / GLA_TAU)
    b_tile = log_a * LOG2_E
    shift = 1
    while shift < n:
        b_tile = b_tile + jnp.where(in_blk >= shift, pltpu.roll(b_tile, shift, axis=0), 0.0)
        shift *= 2
    b_ref[...] = b_tile

    def group(gi, carry):
        g0 = pl.multiple_of(gi * grp, grp)
        b_all = b_ref[pl.ds(g0, grp), :]
        q_all = gq_ref[0, pl.ds(g0, grp), :] * scale
        k_all = gk_ref[0, pl.ds(g0, grp), :]
        v_all = gv_ref[0, pl.ds(g0, grp), :]
        vt_all = gvt_ref[0, :, pl.ds(g0, grp)]
        st = st_ref[...]
        for blk in range(grp // n):
            sl = slice(blk * n, (blk + 1) * n)
            st = _gla_block(q_all[sl], k_all[sl], v_all[sl], vt_all[:, sl], b_all[sl],
                            o_ref.at[0, pl.ds(g0 + blk * n, n), :], st,
                            lane_head, expand, row_i)
        st_ref[...] = st
        return carry

    lax.fori_loop(0, rows // grp, group, 0)


def _gla_block(qb, kb, vb, vtb, b, o_blk, st, lane_head, expand, row_i):
    n, vw = GLA_SUB, GLA_WIDTH
    o_inter = _dot_nt((qb * jnp.exp2(b)).astype(BF16), st.astype(BF16))
    terms = []
    for j in range(n):
        diff = jnp.where(row_i >= j, b - b[j:j + 1, :], -jnp.inf)
        terms.append(qb * kb[j:j + 1, :] * jnp.exp2(diff))
    t_all = jnp.concatenate(terms, axis=0).astype(BF16)
    g_all = _dot(t_all, expand)
    o_intra = jnp.zeros((n, vw), F32)
    for j in range(n):
        o_intra = o_intra + g_all[j * n:(j + 1) * n, :] * vb[j:j + 1, :]
    o_blk[...] = o_inter + o_intra
    b_last = b[n - 1:n, :]
    k_dec = kb * jnp.exp2(b_last - b)
    vtb16 = vtb.astype(BF16)
    upd = jnp.concatenate(
        [_dot(vtb16[h * GLA_DV:(h + 1) * GLA_DV, :],
              jnp.where(lane_head == h, k_dec, 0.0).astype(BF16)) for h in range(GLA_HEADS)],
        axis=0)
    return st * jnp.exp2(b_last) + upd


def _gla(gq, gk, gvt, gv, misc, w_gate2, b_gate2):
    B, L, _ = gq.shape
    rows = min(GLA_ROWS, L)
    row_spec = lambda w: pl.BlockSpec((1, rows, w), lambda b, i: (b, i, 0))
    full = lambda a: pl.BlockSpec(a.shape, lambda b, i: (0,) * a.ndim)
    return pl.pallas_call(
        _gla_body,
        grid=(B, L // rows),
        in_specs=[row_spec(GLA_KW), row_spec(GLA_KW),
                  pl.BlockSpec((1, GLA_WIDTH, rows), lambda b, i: (b, 0, i)),
                  row_spec(GLA_WIDTH), row_spec(MISC_W), full(w_gate2), full(b_gate2)],
        out_specs=row_spec(GLA_WIDTH),
        out_shape=jax.ShapeDtypeStruct((B, L, GLA_WIDTH), F32),
        scratch_shapes=[pltpu.VMEM((GLA_WIDTH, GLA_KW), F32), pltpu.VMEM((rows, GLA_KW), F32)],
        compiler_params=pltpu.CompilerParams(
            dimension_semantics=("parallel", "arbitrary"), vmem_limit_bytes=VMEM_LIMIT),
    )(gq, gk, gvt, gv, misc, w_gate2, b_gate2)


def _rms(x, g):
    ms = jnp.mean(x * x, axis=-1, keepdims=True)
    return (x * lax.rsqrt(ms + NORM_EPS)) * g


def _mix_ffn_body(x_ref, attt_ref, gla_ref, go_ref, g1_ref, sc_ref, sh_ref, g2_ref,
                  ag_ref, gg_ref, ng_ref, fg_ref, wo_ref, wgu_ref, wd_ref, o_ref, *, d_ff, final):
    a = _rms(attt_ref[0].T, ag_ref[...])
    gl = gla_ref[0]
    parts = []
    for h in range(GLA_HEADS):
        sl = slice(h * GLA_DV, (h + 1) * GLA_DV)
        parts.append(_rms(gl[:, sl], gg_ref[...]))
    g = jnp.concatenate(parts, axis=1) * _silu(go_ref[0])
    mix = (_dot(a.astype(BF16), wo_ref[0:ATT_WIDTH, :])
           + _dot(g.astype(BF16), wo_ref[ATT_WIDTH:, :]))
    x = x_ref[0] + g1_ref[0] * mix
    h = _rms(x, ng_ref[...]) * (1.0 + sc_ref[0]) + sh_ref[0]
    gu = _dot(h.astype(BF16), wgu_ref[...])
    act = _silu(gu[:, :d_ff]) * gu[:, d_ff:]
    y = x + g2_ref[0] * _dot(act.astype(BF16), wd_ref[...])
    o_ref[0] = _rms(y, fg_ref[...]) if final else y


def _mix_ffn(x, att_t, gla, go, g1, sc2, sh2, g2, att_out_g, gla_out_g, norm2_g, final_g,
             w_out, w_gate_up, w_down, final):
    B, L, D = x.shape
    d_ff = w_down.shape[0]
    tm = min(ROWS_FFN, L)
    row_spec = lambda w: pl.BlockSpec((1, tm, w), lambda b, i: (b, i, 0))
    full = lambda a: pl.BlockSpec(a.shape, lambda b, i: (0,) * a.ndim,
                                  pipeline_mode=pl.Buffered(1))
    vec = pl.BlockSpec((1, 1, D), lambda b, i: (b, 0, 0))
    wo = w_out.astype(BF16)
    wgu = w_gate_up.astype(BF16)
    wd = w_down.astype(BF16)
    return pl.pallas_call(
        functools.partial(_mix_ffn_body, d_ff=d_ff, final=final),
        grid=(B, L // tm),
        in_specs=[row_spec(D), pl.BlockSpec((1, ATT_WIDTH, tm), lambda b, i: (b, 0, i)),
                  row_spec(GLA_WIDTH), row_spec(GLA_WIDTH), vec, vec, vec, vec,
                  full(att_out_g), full(gla_out_g), full(norm2_g), full(final_g),
                  full(wo), full(wgu), full(wd)],
        out_specs=row_spec(D),
        out_shape=jax.ShapeDtypeStruct((B, L, D), F32),
        compiler_params=pltpu.CompilerParams(
            dimension_semantics=("parallel", "parallel"), vmem_limit_bytes=VMEM_LIMIT),
    )(x, att_t, gla, go, g1, sc2, sh2, g2, att_out_g, gla_out_g, norm2_g, final_g, wo, wgu, wd)


def kernel(x, c, w_mod, b_mod, norm1_g, w_in, w_gate2, b_gate2, att_out_g, gla_out_g,
           w_out, norm2_g, w_gate_up, w_down, final_g):
    B, L, D = x.shape
    depth = w_mod.shape[0]
    for l in range(depth):
        mod = _modulation(c, w_mod[l], b_mod[l][None, :])
        sh1, sc1, g1, sh2, sc2, g2 = [m[:, None, :] for m in jnp.split(mod, N_MOD, axis=-1)]
        qt, k, qit, misc, ki, gq, gk, gv, go, vt, gvt, misct = _input_projection(
            x, sc1, sh1, norm1_g[l][None, :], w_in[l])
        att_t = _dsa_attention(qt, k, vt, qit, ki, misct)
        gla = _gla(gq, gk, gvt, gv, misc, w_gate2[l], b_gate2[l][None, :])
        x = _mix_ffn(x, att_t, gla, go, g1, sc2, sh2, g2, att_out_g[l][None, :],
                     gla_out_g[l][None, :], norm2_g[l][None, :], final_g[None, :],
                     w_out[l], w_gate_up[l], w_down[l], final=(l == depth - 1))
    return x
```

```python
import functools

import jax
import jax.numpy as jnp
from jax import lax
from jax.experimental import pallas as pl
from jax.experimental.pallas import tpu as pltpu

F32 = jnp.float32
BF16 = jnp.bfloat16
I32 = jnp.int32
HIGHEST = lax.Precision.HIGHEST

NORM_EPS = 1e-6
ATT_HEADS = 4
ATT_HEAD_DIM = 128
ATT_WIDTH = ATT_HEADS * ATT_HEAD_DIM
IDX_HEADS = 8
IDX_DIM = 64
TOPK_MAX = 256
GLA_HEADS = 4
GLA_DK = 64
GLA_DV = 128
GLA_KW = GLA_HEADS * GLA_DK
GLA_WIDTH = GLA_HEADS * GLA_DV
GLA_GATE_RANK = 16
GLA_TAU = 16.0
N_MOD = 6

LANES = 128
SUBLANES = 8
PACKED_ROWS = 16
MISC_W = LANES
W_IDX_OFF = IDX_DIM
G_LR_OFF = IDX_DIM + IDX_HEADS

INT_MIN = -2 ** 31
KEY_MASKED = INT_MIN
MASKED_LOGIT = -1e30
LOG2_E = 1.4426950408889634
VMEM_LIMIT = 56 * 1024 * 1024

ROWS_PROJ = 512
ROWS_FFN = 512
Q_TILE = 256
K_CHUNK = 512
WORD_BITS = 32
GROUP_KEYS = WORD_BITS * SUBLANES
GROUPS_PER_CHUNK = K_CHUNK // GROUP_KEYS
SEARCH_GROUPS = 8
GLA_SUB = 16
GLA_GROUP = 128
GLA_ROWS = 512


def _dot(a, b):
    return jnp.dot(a, b, preferred_element_type=F32)


def _dot_nt(a, b):
    return lax.dot_general(a, b, (((1,), (1,)), ((), ())), preferred_element_type=F32)


def _silu(x):
    return x * jax.nn.sigmoid(x)


def _mod_body(c_ref, w_ref, b_ref, o_ref):
    s = _silu(c_ref[...])
    o_ref[...] = jnp.dot(s, w_ref[...], precision=HIGHEST,
                         preferred_element_type=F32) + b_ref[...]


def _modulation(c, w_mod, b_mod):
    B, D = c.shape
    N = w_mod.shape[1]
    tn = D
    return pl.pallas_call(
        _mod_body,
        grid=(N // tn,),
        in_specs=[pl.BlockSpec((B, D), lambda j: (0, 0)),
                  pl.BlockSpec((D, tn), lambda j: (0, j)),
                  pl.BlockSpec((1, tn), lambda j: (0, j))],
        out_specs=pl.BlockSpec((B, tn), lambda j: (0, j)),
        out_shape=jax.ShapeDtypeStruct((B, N), F32),
    )(c, w_mod, b_mod)


def _inproj_body(x_ref, sc_ref, sh_ref, g_ref, wa_ref, wm_ref, wg_ref, wt_ref,
                 qt_o, k_o, qit_o, misc_o, ki_o, gq_o, gk_o, gv_o, go_o, vt_o, gvt_o, misct_o):
    x = x_ref[0]
    ms = jnp.mean(x * x, axis=-1, keepdims=True)
    y = (x * lax.rsqrt(ms + NORM_EPS)) * g_ref[...]
    h = y * (1.0 + sc_ref[0]) + sh_ref[0]
    hb = h.astype(BF16)
    w = ATT_WIDTH
    k_o[0] = _dot(hb, wa_ref[...]).astype(BF16)
    pm = _dot(hb, wm_ref[...])
    misc_o[0] = pm
    ki_o[0] = pm[:, 0:IDX_DIM].astype(BF16)
    pg = _dot(hb, wg_ref[...])
    gq_o[0] = pg[:, 0:GLA_KW]
    gk_o[0] = pg[:, GLA_KW:2 * GLA_KW]
    gv_o[0] = pg[:, 2 * GLA_KW:2 * GLA_KW + GLA_WIDTH]
    go_o[0] = pg[:, 2 * GLA_KW + GLA_WIDTH:]
    pt = _dot_nt(wt_ref[...], hb)
    qt_o[0] = (pt[0:w, :] * (ATT_HEAD_DIM ** -0.5 * LOG2_E)).astype(BF16)
    qit_o[0] = pt[w:2 * w, :].astype(BF16)
    vt_o[0] = pt[2 * w:3 * w, :].astype(BF16)
    gvt_o[0] = pt[3 * w:3 * w + GLA_WIDTH, :]
    misct_o[0] = pt[3 * w + GLA_WIDTH:, :]


def _input_projection(x, sc1, sh1, norm1_g, w_in):
    B, L, D = x.shape
    tm = min(ROWS_PROJ, L)
    o = [0]
    for s in (ATT_WIDTH, ATT_WIDTH, ATT_WIDTH, IDX_HEADS * IDX_DIM, IDX_DIM, IDX_HEADS,
              GLA_KW, GLA_KW, GLA_WIDTH, GLA_WIDTH, GLA_GATE_RANK):
        o.append(o[-1] + s)
    col = lambda i: w_in[:, o[i]:o[i + 1]]
    wa = col(1).astype(BF16)
    pad = jnp.zeros((D, MISC_W - IDX_DIM - IDX_HEADS - GLA_GATE_RANK), w_in.dtype)
    wm = jnp.concatenate([col(4), col(5), col(10), pad], axis=1).astype(BF16)
    wg = jnp.concatenate([col(6), col(7), col(8), col(9)], axis=1).astype(BF16)
    wt = jnp.concatenate([col(0), col(3), col(2), col(8), wm], axis=1).T.astype(BF16)

    row = lambda w, dt: jax.ShapeDtypeStruct((B, L, w), dt)
    feat = lambda w, dt: jax.ShapeDtypeStruct((B, w, L), dt)
    row_spec = lambda w: pl.BlockSpec((1, tm, w), lambda b, i: (b, i, 0))
    feat_spec = lambda w: pl.BlockSpec((1, w, tm), lambda b, i: (b, 0, i))
    full = lambda a: pl.BlockSpec(a.shape, lambda b, i: (0,) * a.ndim,
                                  pipeline_mode=pl.Buffered(1))
    vec = pl.BlockSpec((1, 1, D), lambda b, i: (b, 0, 0))
    return pl.pallas_call(
        _inproj_body,
        grid=(B, L // tm),
        in_specs=[row_spec(D), vec, vec, full(norm1_g), full(wa), full(wm), full(wg), full(wt)],
        out_specs=[feat_spec(ATT_WIDTH), row_spec(ATT_WIDTH),
                   feat_spec(IDX_HEADS * IDX_DIM), row_spec(MISC_W), row_spec(IDX_DIM),
                   row_spec(GLA_KW), row_spec(GLA_KW), row_spec(GLA_WIDTH), row_spec(GLA_WIDTH),
                   feat_spec(ATT_WIDTH), feat_spec(GLA_WIDTH), feat_spec(MISC_W)],
        out_shape=[feat(ATT_WIDTH, BF16), row(ATT_WIDTH, BF16),
                   feat(IDX_HEADS * IDX_DIM, BF16), row(MISC_W, F32), row(IDX_DIM, BF16),
                   row(GLA_KW, F32), row(GLA_KW, F32), row(GLA_WIDTH, F32), row(GLA_WIDTH, F32),
                   feat(ATT_WIDTH, BF16), feat(GLA_WIDTH, F32), feat(MISC_W, F32)],
        compiler_params=pltpu.CompilerParams(
            dimension_semantics=("parallel", "parallel"), vmem_limit_bytes=VMEM_LIMIT),
    )(x, sc1, sh1, norm1_g, wa, wm, wg, wt)


def _bit_transpose(words):
    x = list(words)
    for shift, mask in ((16, 0x0000FFFF), (8, 0x00FF00FF), (4, 0x0F0F0F0F),
                        (2, 0x33333333), (1, 0x55555555)):
        for lo in range(WORD_BITS):
            if lo & shift == 0:
                a, b = x[lo], x[lo + shift]
                t = (a ^ lax.shift_right_logical(b, shift)) & mask
                x[lo] = a ^ t
                x[lo + shift] = b ^ lax.shift_left(t, shift)
    return x


def _dsa_body(qit_ref, misct_ref, ki_ref, qt_ref, k_ref, vt_ref, o_ref,
              keys_ref, planes_ref, cand_ref, dta_ref, dtb_ref, qka_ref, qkb_ref,
              mxa_ref, mxb_ref, m_ref, acc_ref, *, n_sel, seq_len):
    i = pl.program_id(1)
    tq, ck = Q_TILE, K_CHUNK
    n_full = (i * tq) // ck
    n_chunks = n_full + 1
    idx_scale = (IDX_HEADS ** -0.5) * (IDX_DIM ** -0.5)

    s_iota = lax.broadcasted_iota(I32, (ck, tq), 0)
    t_pos = i * tq + lax.broadcasted_iota(I32, (ck, tq), 1)
    w = misct_ref[0][W_IDX_OFF:W_IDX_OFF + IDX_HEADS, :] * idx_scale
    qit_heads = [qit_ref[0, h * IDX_DIM:(h + 1) * IDX_DIM, :] for h in range(IDX_HEADS)]
    w_rows = [w[h:h + 1, :] for h in range(IDX_HEADS)]

    half = GROUP_KEYS
    s_half = lax.broadcasted_iota(I32, (half, tq), 0)
    t_half = i * tq + lax.broadcasted_iota(I32, (half, tq), 1)

    def dots_to(buf, u):
        kc = ki_ref[0, pl.ds(pl.multiple_of(u * half, half), half), :]
        for h in range(IDX_HEADS):
            buf[h] = _dot(kc, qit_heads[h])

    def keys_from(buf, c, second, diagonal):
        u = 2 * c + second
        score = jnp.zeros((half, tq), F32)
        for h in range(IDX_HEADS):
            score = score + jnp.maximum(buf[h], 0.0) * w_rows[h]
        bits = lax.bitcast_convert_type(score, I32)
        key = jnp.where(bits < 0, INT_MIN - bits, bits)
        if diagonal:
            key = jnp.where(u * half + s_half <= t_half, key, KEY_MASKED)
        keys_ref[c, second * half:(second + 1) * half, :] = key

    dots_to(dta_ref, 0)

    def score_pair(c, carry):
        dots_to(dtb_ref, 2 * c + 1)
        keys_from(dta_ref, c, 0, diagonal=False)
        dots_to(dta_ref, 2 * c + 2)
        keys_from(dtb_ref, c, 1, diagonal=False)
        return carry

    lax.fori_loop(0, n_full, score_pair, 0)
    n_halves = GROUPS_PER_CHUNK * n_chunks

    @pl.when(i % 2 == 0)
    def _():
        keys_from(dta_ref, n_full, 0, diagonal=True)
        keys_ref[n_full, half:, :] = jnp.full((half, tq), KEY_MASKED, I32)

    @pl.when(i % 2 == 1)
    def _():
        dots_to(dtb_ref, i)
        keys_from(dta_ref, n_full, 0, diagonal=False)
        keys_from(dtb_ref, n_full, 1, diagonal=True)

    def slice_chunk(c, carry):
        for second in range(GROUPS_PER_CHUNK):
            u = GROUPS_PER_CHUNK * c + second
            for lb in range(tq // LANES):
                lanes = slice(lb * LANES, (lb + 1) * LANES)
                rows = [keys_ref[c, second * half + m * SUBLANES:
                                 second * half + (m + 1) * SUBLANES, lanes] ^ INT_MIN
                        for m in range(WORD_BITS)]
                for p, plane in enumerate(_bit_transpose(rows)):
                    planes_ref[p, u, :, lanes] = plane
            cand_ref[u] = jnp.full((SUBLANES, tq), -1, I32)
        return carry

    lax.fori_loop(0, n_chunks, slice_chunk, 0)

    sg = SEARCH_GROUPS
    n_steps = (n_halves + sg - 1) // sg

    def pad_group(g, carry):
        planes_ref[:, g] = jnp.zeros((WORD_BITS, SUBLANES, tq), I32)
        cand_ref[g] = jnp.zeros((SUBLANES, tq), I32)
        return carry

    lax.fori_loop(n_halves, sg * n_steps, pad_group, 0)

    def count_over_groups(words_of):
        def body(step, acc):
            words = words_of(pl.multiple_of(step * sg, sg))
            return acc + jnp.sum(lax.population_count(words).astype(F32), axis=0)
        acc = lax.fori_loop(0, n_steps, body, jnp.zeros((SUBLANES, tq), F32))
        return jnp.sum(acc, axis=0, keepdims=True)

    def decide(cnt, rank, kth_u):
        take = cnt >= rank
        return (jnp.where(take, 0, -1), jnp.where(take, rank, rank - cnt),
                lax.shift_left(kth_u, 1) | jnp.where(take, 1, 0))

    carry = decide(
        count_over_groups(lambda g0: cand_ref[pl.ds(g0, sg)] & planes_ref[0, pl.ds(g0, sg)]),
        jnp.full((1, tq), float(n_sel), F32), jnp.zeros((1, tq), I32))

    def search_pass(p, carry):
        flip, rank, kth_u = carry

        def candidates_with_bit(g0):
            gs = pl.ds(g0, sg)
            cand = cand_ref[gs] & (planes_ref[p - 1, gs] ^ flip)
            cand_ref[gs] = cand
            return cand & planes_ref[p, gs]

        return decide(count_over_groups(candidates_with_bit), rank, kth_u)

    flip, rank, kth_u = lax.fori_loop(1, WORD_BITS, search_pass, carry)

    def equal_to_threshold(g0):
        gs = pl.ds(g0, sg)
        cand = cand_ref[gs] & (planes_ref[WORD_BITS - 1, gs] ^ flip)
        cand_ref[gs] = cand
        return cand

    n_equal = count_over_groups(equal_to_threshold)
    kth = kth_u ^ INT_MIN
    has_tie = (n_equal > rank) & (kth > KEY_MASKED)
    kth = jnp.maximum(kth, KEY_MASKED + 1)

    @pl.when(jnp.max(jnp.where(has_tie, 1.0, 0.0)) > 0.0)
    def _():
        grp = lax.broadcasted_iota(I32, (sg, SUBLANES, tq), 0)
        sub = lax.broadcasted_iota(I32, (sg, SUBLANES, tq), 1)
        sub_bits = SUBLANES.bit_length() - 1

        def ties_before(bound):
            def words_of(g0):
                first = (g0 + grp) * GROUP_KEYS + sub
                n_low = lax.shift_right_arithmetic(bound - first + (SUBLANES - 1), sub_bits)
                n_low = jnp.clip(n_low, 0, WORD_BITS)
                mask = jnp.where(n_low > 0,
                                 lax.shift_left(jnp.int32(-1), WORD_BITS - jnp.maximum(n_low, 1)), 0)
                return cand_ref[pl.ds(g0, sg)] & mask
            return count_over_groups(words_of)

        n_bits = (seq_len - 1).bit_length()

        def idx_step(it, lo):
            cand = lo + lax.shift_left(jnp.int32(1), n_bits - 1 - it)
            return jnp.where(ties_before(cand) < rank, cand, lo)

        last = lax.fori_loop(0, n_bits, idx_step, jnp.zeros((1, tq), I32))
        last_kept = jnp.where(has_tie, last, seq_len)

        def demote(c, carry):
            kk = keys_ref[c]
            s_pos = c * ck + s_iota
            drop = (kk == kth) & (s_pos > last_kept)
            keys_ref[c] = jnp.where(drop, kk - 1, kk)
            return carry

        lax.fori_loop(0, n_chunks, demote, 0)

    hd = ATT_HEAD_DIM
    qt_heads = [qt_ref[0, h * hd:(h + 1) * hd, :] for h in range(ATT_HEADS)]
    ones_rows = jnp.ones((PACKED_ROWS, ck), BF16)

    heads = [slice(h * hd, (h + 1) * hd) for h in range(ATT_HEADS)]

    def qk_to(slot, c):
        buf, cmax = slot
        kc = k_ref[0, pl.ds(pl.multiple_of(c * ck, ck), ck), :]
        bias = jnp.where(keys_ref[c] >= kth, 0.0, MASKED_LOGIT)
        for h, sl in enumerate(heads):
            logits = _dot(kc[:, sl], qt_heads[h]) + bias
            buf[h] = logits
            cmax[h] = jnp.max(logits, axis=0, keepdims=True)

    def softmax_pv(c, slot):
        buf, cmax = slot
        start = pl.multiple_of(c * ck, ck)
        ms = [m_ref[h] for h in range(ATT_HEADS)]
        new_ms = [jnp.maximum(ms[h], cmax[h]) for h in range(ATT_HEADS)]
        ps = [jnp.exp2(buf[h] - new_ms[h]).astype(BF16) for h in range(ATT_HEADS)]
        for h, sl in enumerate(heads):
            vt_ext = jnp.concatenate([vt_ref[0, sl, pl.ds(start, ck)], ones_rows], axis=0)
            acc_ref[h] = jnp.exp2(ms[h] - new_ms[h]) * acc_ref[h] + _dot(vt_ext, ps[h])
            m_ref[h] = new_ms[h]

    slot_a, slot_b = (qka_ref, mxa_ref), (qkb_ref, mxb_ref)
    m_ref[...] = jnp.full_like(m_ref, MASKED_LOGIT)
    acc_ref[...] = jnp.zeros_like(acc_ref)
    qk_to(slot_a, 0)

    def attend_pair(j, carry):
        c = 2 * j
        qk_to(slot_b, c + 1)
        softmax_pv(c, slot_a)
        qk_to(slot_a, c + 2)
        softmax_pv(c + 1, slot_b)
        return carry

    lax.fori_loop(0, n_full // 2, attend_pair, 0)

    @pl.when(n_full % 2 == 0)
    def _():
        softmax_pv(n_full, slot_a)

    @pl.when(n_full % 2 == 1)
    def _():
        qk_to(slot_b, n_full)
        softmax_pv(n_full - 1, slot_a)
        softmax_pv(n_full, slot_b)

    for h in range(ATT_HEADS):
        o_ref[0, h * hd:(h + 1) * hd, :] = acc_ref[h, :hd, :] / acc_ref[h, hd:hd + 1, :]


def _dsa_attention(qt, k, vt, qit, ki, misct):
    B, L, _ = k.shape
    tq = Q_TILE
    assert tq == GROUP_KEYS and K_CHUNK == 2 * tq and L % K_CHUNK == 0
    n_sel = min(TOPK_MAX, L // 4)
    assert n_sel <= K_CHUNK
    body = functools.partial(_dsa_body, n_sel=n_sel, seq_len=L)
    feat_spec = lambda w: pl.BlockSpec((1, w, tq), lambda b, i: (b, 0, i))
    seq_spec = lambda a: pl.BlockSpec((1,) + a.shape[1:], lambda b, i: (b, 0, 0),
                                      pipeline_mode=pl.Buffered(1))
    n_groups = pl.cdiv(GROUPS_PER_CHUNK * (L // K_CHUNK), SEARCH_GROUPS) * SEARCH_GROUPS
    return pl.pallas_call(
        body,
        grid=(B, L // tq),
        in_specs=[feat_spec(IDX_HEADS * IDX_DIM), feat_spec(MISC_W), seq_spec(ki),
                  feat_spec(ATT_WIDTH), seq_spec(k), seq_spec(vt)],
        out_specs=feat_spec(ATT_WIDTH),
        out_shape=jax.ShapeDtypeStruct((B, ATT_WIDTH, L), F32),
        scratch_shapes=[
            pltpu.VMEM((L // K_CHUNK, K_CHUNK, tq), I32),
            pltpu.VMEM((WORD_BITS, n_groups, SUBLANES, tq), I32),
            pltpu.VMEM((n_groups, SUBLANES, tq), I32),
            pltpu.VMEM((IDX_HEADS, GROUP_KEYS, tq), F32), pltpu.VMEM((IDX_HEADS, GROUP_KEYS, tq), F32),
            pltpu.VMEM((ATT_HEADS, K_CHUNK, tq), F32), pltpu.VMEM((ATT_HEADS, K_CHUNK, tq), F32),
            pltpu.VMEM((ATT_HEADS, 1, tq), F32), pltpu.VMEM((ATT_HEADS, 1, tq), F32),
            pltpu.VMEM((ATT_HEADS, 1, tq), F32),
            pltpu.VMEM((ATT_HEADS, ATT_HEAD_DIM + PACKED_ROWS, tq), F32)],
        compiler_params=pltpu.CompilerParams(
            dimension_semantics=("parallel", "arbitrary"), vmem_limit_bytes=VMEM_LIMIT),
    )(qit, misct, ki, qt, k, vt)


def _gla_body(gq_ref, gk_ref, gvt_ref, gv_ref, misc_ref, w2_ref, b2_ref, o_ref, st_ref, b_ref):
    n, kw, vw = GLA_SUB, GLA_KW, GLA_WIDTH

    @pl.when(pl.program_id(1) == 0)
    def _():
        st_ref[...] = jnp.zeros_like(st_ref)

    rows = gq_ref.shape[1]
    grp = GLA_GROUP
    lane_head = lax.broadcasted_iota(I32, (n, kw), 1) // GLA_DK
    ed = lax.broadcasted_iota(I32, (kw, vw), 0) // GLA_DK
    ev = lax.broadcasted_iota(I32, (kw, vw), 1) // GLA_DV
    expand = jnp.where(ed == ev, 1.0, 0.0).astype(BF16)
    row_i = lax.broadcasted_iota(I32, (n, kw), 0)
    in_blk = lax.broadcasted_iota(I32, (rows, kw), 0) % n
    scale = GLA_DK ** -0.5

    g_lr = misc_ref[0][:, G_LR_OFF:G_LR_OFF + GLA_GATE_RANK]
    z = jnp.dot(g_lr, w2_ref[...], precision=HIGHEST, preferred_element_type=F32) + b2_ref[...]
    log_a = (jnp.minimum(z, 0.0) - jnp.log1p(jnp.exp(-jnp.abs(z)))) * (1.0 / GLA_TAU)
    b_tile = log_a * LOG2_E
    shift = 1
    while shift < n:
        b_tile = b_tile + jnp.where(in_blk >= shift, pltpu.roll(b_tile, shift, axis=0), 0.0)
        shift *= 2
    b_ref[...] = b_tile

    def group(gi, carry):
        g0 = pl.multiple_of(gi * grp, grp)
        b_all = b_ref[pl.ds(g0, grp), :]
        q_all = gq_ref[0, pl.ds(g0, grp), :] * scale
        k_all = gk_ref[0, pl.ds(g0, grp), :]
        v_all = gv_ref[0, pl.ds(g0, grp), :]
        vt_all = gvt_ref[0, :, pl.ds(g0, grp)]
        st = st_ref[...]
        for blk in range(grp // n):
            sl = slice(blk * n, (blk + 1) * n)
            st = _gla_block(q_all[sl], k_all[sl], v_all[sl], vt_all[:, sl], b_all[sl],
                            o_ref.at[0, pl.ds(g0 + blk * n, n), :], st,
                            lane_head, expand, row_i)
        st_ref[...] = st
        return carry

    lax.fori_loop(0, rows // grp, group, 0)


def _gla_block(qb, kb, vb, vtb, b, o_blk, st, lane_head, expand, row_i):
    n, vw = GLA_SUB, GLA_WIDTH
    o_inter = _dot_nt((qb * jnp.exp2(b)).astype(BF16), st.astype(BF16))
    terms = []
    for j in range(n):
        diff = jnp.where(row_i >= j, b - b[j:j + 1, :], -jnp.inf)
        terms.append(qb * kb[j:j + 1, :] * jnp.exp2(diff))
    t_all = jnp.concatenate(terms, axis=0).astype(BF16)
    g_all = _dot(t_all, expand)
    o_intra = jnp.zeros((n, vw), F32)
    for j in range(n):
        o_intra = o_intra + g_all[j * n:(j + 1) * n, :] * vb[j:j + 1, :]
    o_blk[...] = o_inter + o_intra
    b_last = b[n - 1:n, :]
    k_dec = kb * jnp.exp2(b_last - b)
    vtb16 = vtb.astype(BF16)
    upd = jnp.concatenate(
        [_dot(vtb16[h * GLA_DV:(h + 1) * GLA_DV, :],
              jnp.where(lane_head == h, k_dec, 0.0).astype(BF16)) for h in range(GLA_HEADS)],
        axis=0)
    return st * jnp.exp2(b_last) + upd


def _gla(gq, gk, gvt, gv, misc, w_gate2, b_gate2):
    B, L, _ = gq.shape
    rows = min(GLA_ROWS, L)
    row_spec = lambda w: pl.BlockSpec((1, rows, w), lambda b, i: (b, i, 0))
    full = lambda a: pl.BlockSpec(a.shape, lambda b, i: (0,) * a.ndim)
    return pl.pallas_call(
        _gla_body,
        grid=(B, L // rows),
        in_specs=[row_spec(GLA_KW), row_spec(GLA_KW),
                  pl.BlockSpec((1, GLA_WIDTH, rows), lambda b, i: (b, 0, i)),
                  row_spec(GLA_WIDTH), row_spec(MISC_W), full(w_gate2), full(b_gate2)],
        out_specs=row_spec(GLA_WIDTH),
        out_shape=jax.ShapeDtypeStruct((B, L, GLA_WIDTH), F32),
        scratch_shapes=[pltpu.VMEM((GLA_WIDTH, GLA_KW), F32), pltpu.VMEM((rows, GLA_KW), F32)],
        compiler_params=pltpu.CompilerParams(
            dimension_semantics=("parallel", "arbitrary"), vmem_limit_bytes=VMEM_LIMIT),
    )(gq, gk, gvt, gv, misc, w_gate2, b_gate2)


def _rms(x, g):
    ms = jnp.mean(x * x, axis=-1, keepdims=True)
    return (x * lax.rsqrt(ms + NORM_EPS)) * g


def _mix_ffn_body(x_ref, attt_ref, gla_ref, go_ref, g1_ref, sc_ref, sh_ref, g2_ref,
                  ag_ref, gg_ref, ng_ref, fg_ref, wo_ref, wgu_ref, wd_ref, o_ref, *, d_ff, final):
    rows = x_ref.shape[1]
    halves = [slice(r * rows // 2, (r + 1) * rows // 2) for r in range(2)]

    def mixed(rs):
        a = _rms(attt_ref[0, :, rs].T, ag_ref[...])
        gl = gla_ref[0, rs, :]
        parts = []
        for h in range(GLA_HEADS):
            sl = slice(h * GLA_DV, (h + 1) * GLA_DV)
            parts.append(_rms(gl[:, sl], gg_ref[...]))
        g = jnp.concatenate(parts, axis=1) * _silu(go_ref[0, rs, :])
        mix = (_dot(a.astype(BF16), wo_ref[0:ATT_WIDTH, :])
               + _dot(g.astype(BF16), wo_ref[ATT_WIDTH:, :]))
        return x_ref[0, rs, :] + g1_ref[0] * mix

    xs = [mixed(rs) for rs in halves]
    hs = [(_rms(x, ng_ref[...]) * (1.0 + sc_ref[0]) + sh_ref[0]).astype(BF16) for x in xs]
    gus = [_dot(h, wgu_ref[...]) for h in hs]
    acts = [(_silu(gu[:, :d_ff]) * gu[:, d_ff:]).astype(BF16) for gu in gus]
    for rs, x, act in zip(halves, xs, acts):
        y = x + g2_ref[0] * _dot(act, wd_ref[...])
        o_ref[0, rs, :] = _rms(y, fg_ref[...]) if final else y


def _mix_ffn(x, att_t, gla, go, g1, sc2, sh2, g2, att_out_g, gla_out_g, norm2_g, final_g,
             w_out, w_gate_up, w_down, final):
    B, L, D = x.shape
    d_ff = w_down.shape[0]
    tm = min(ROWS_FFN, L)
    row_spec = lambda w: pl.BlockSpec((1, tm, w), lambda b, i: (b, i, 0))
    full = lambda a: pl.BlockSpec(a.shape, lambda b, i: (0,) * a.ndim,
                                  pipeline_mode=pl.Buffered(1))
    vec = pl.BlockSpec((1, 1, D), lambda b, i: (b, 0, 0))
    wo = w_out.astype(BF16)
    wgu = w_gate_up.astype(BF16)
    wd = w_down.astype(BF16)
    return pl.pallas_call(
        functools.partial(_mix_ffn_body, d_ff=d_ff, final=final),
        grid=(B, L // tm),
        in_specs=[row_spec(D), pl.BlockSpec((1, ATT_WIDTH, tm), lambda b, i: (b, 0, i)),
                  row_spec(GLA_WIDTH), row_spec(GLA_WIDTH), vec, vec, vec, vec,
                  full(att_out_g), full(gla_out_g), full(norm2_g), full(final_g),
                  full(wo), full(wgu), full(wd)],
        out_specs=row_spec(D),
        out_shape=jax.ShapeDtypeStruct((B, L, D), F32),
        compiler_params=pltpu.CompilerParams(
            dimension_semantics=("parallel", "parallel"), vmem_limit_bytes=VMEM_LIMIT),
    )(x, att_t, gla, go, g1, sc2, sh2, g2, att_out_g, gla_out_g, norm2_g, final_g, wo, wgu, wd)


def kernel(x, c, w_mod, b_mod, norm1_g, w_in, w_gate2, b_gate2, att_out_g, gla_out_g,
           w_out, norm2_g, w_gate_up, w_down, final_g):
    B, L, D = x.shape
    depth = w_mod.shape[0]
    for l in range(depth):
        mod = _modulation(c, w_mod[l], b_mod[l][None, :])
        sh1, sc1, g1, sh2, sc2, g2 = [m[:, None, :] for m in jnp.split(mod, N_MOD, axis=-1)]
        qt, k, qit, misc, ki, gq, gk, gv, go, vt, gvt, misct = _input_projection(
            x, sc1, sh1, norm1_g[l][None, :], w_in[l])
        att_t = _dsa_attention(qt, k, vt, qit, ki, misct)
        gla = _gla(gq, gk, gvt, gv, misc, w_gate2[l], b_gate2[l][None, :])
        x = _mix_ffn(x, att_t, gla, go, g1, sc2, sh2, g2, att_out_g[l][None, :],
                     gla_out_g[l][None, :], norm2_g[l][None, :], final_g[None, :],
                     w_out[l], w_gate_up[l], w_down[l], final=(l == depth - 1))
    return x
```

```python
import functools

import jax
import jax.numpy as jnp
from jax import lax
from jax.experimental import pallas as pl
from jax.experimental.pallas import tpu as pltpu

F32 = jnp.float32
BF16 = jnp.bfloat16
I32 = jnp.int32
HIGHEST = lax.Precision.HIGHEST

NORM_EPS = 1e-6
ATT_HEADS = 4
ATT_HEAD_DIM = 128
ATT_WIDTH = ATT_HEADS * ATT_HEAD_DIM
IDX_HEADS = 8
IDX_DIM = 64
TOPK_MAX = 256
GLA_HEADS = 4
GLA_DK = 64
GLA_DV = 128
GLA_KW = GLA_HEADS * GLA_DK
GLA_WIDTH = GLA_HEADS * GLA_DV
GLA_GATE_RANK = 16
GLA_TAU = 16.0
N_MOD = 6

LANES = 128
SUBLANES = 8
PACKED_ROWS = 16
MISC_W = LANES
W_IDX_OFF = IDX_DIM
G_LR_OFF = IDX_DIM + IDX_HEADS

INT_MIN = -2 ** 31
KEY_MASKED = INT_MIN
MASKED_LOGIT = -1e30
LOG2_E = 1.4426950408889634
VMEM_LIMIT = 56 * 1024 * 1024

ROWS_PROJ = 512
ROWS_FFN = 512
Q_TILE = 256
K_CHUNK = 512
WORD_BITS = 32
GROUP_KEYS = WORD_BITS * SUBLANES
GROUPS_PER_CHUNK = K_CHUNK // GROUP_KEYS
SEARCH_GROUPS = 8
GLA_SUB = 16
GLA_GROUP = 128
GLA_ROWS = 512


def _dot(a, b):
    return jnp.dot(a, b, preferred_element_type=F32)


def _dot_nt(a, b):
    return lax.dot_general(a, b, (((1,), (1,)), ((), ())), preferred_element_type=F32)


def _silu(x):
    return x * jax.nn.sigmoid(x)


def _mod_body(c_ref, w_ref, b_ref, o_ref):
    s = _silu(c_ref[...])
    o_ref[...] = jnp.dot(s, w_ref[...], precision=HIGHEST,
                         preferred_element_type=F32) + b_ref[...]


def _modulation(c, w_mod, b_mod):
    B, D = c.shape
    N = w_mod.shape[1]
    tn = D
    return pl.pallas_call(
        _mod_body,
        grid=(N // tn,),
        in_specs=[pl.BlockSpec((B, D), lambda j: (0, 0)),
                  pl.BlockSpec((D, tn), lambda j: (0, j)),
                  pl.BlockSpec((1, tn), lambda j: (0, j))],
        out_specs=pl.BlockSpec((B, tn), lambda j: (0, j)),
        out_shape=jax.ShapeDtypeStruct((B, N), F32),
    )(c, w_mod, b_mod)


def _inproj_body(x_ref, sc_ref, sh_ref, g_ref, wa_ref, wm_ref, wg_ref, wt_ref,
                 qt_o, k_o, qit_o, misc_o, ki_o, gq_o, gk_o, gv_o, go_o, vt_o, gvt_o, misct_o):
    x = x_ref[0]
    ms = jnp.mean(x * x, axis=-1, keepdims=True)
    y = (x * lax.rsqrt(ms + NORM_EPS)) * g_ref[...]
    h = y * (1.0 + sc_ref[0]) + sh_ref[0]
    hb = h.astype(BF16)
    w = ATT_WIDTH
    k_o[0] = _dot(hb, wa_ref[...]).astype(BF16)
    pm = _dot(hb, wm_ref[...])
    misc_o[0] = pm
    ki_o[0] = pm[:, 0:IDX_DIM].astype(BF16)
    pg = _dot(hb, wg_ref[...])
    gq_o[0] = pg[:, 0:GLA_KW]
    gk_o[0] = pg[:, GLA_KW:2 * GLA_KW]
    gv_o[0] = pg[:, 2 * GLA_KW:2 * GLA_KW + GLA_WIDTH]
    go_o[0] = pg[:, 2 * GLA_KW + GLA_WIDTH:]
    pt = _dot_nt(wt_ref[...], hb)
    qt_o[0] = (pt[0:w, :] * (ATT_HEAD_DIM ** -0.5 * LOG2_E)).astype(BF16)
    qit_o[0] = pt[w:2 * w, :].astype(BF16)
    vt_o[0] = pt[2 * w:3 * w, :].astype(BF16)
    gvt_o[0] = pt[3 * w:3 * w + GLA_WIDTH, :]
    misct_o[0] = pt[3 * w + GLA_WIDTH:, :]


def _input_projection(x, sc1, sh1, norm1_g, w_in):
    B, L, D = x.shape
    tm = min(ROWS_PROJ, L)
    o = [0]
    for s in (ATT_WIDTH, ATT_WIDTH, ATT_WIDTH, IDX_HEADS * IDX_DIM, IDX_DIM, IDX_HEADS,
              GLA_KW, GLA_KW, GLA_WIDTH, GLA_WIDTH, GLA_GATE_RANK):
        o.append(o[-1] + s)
    col = lambda i: w_in[:, o[i]:o[i + 1]]
    wa = col(1).astype(BF16)
    pad = jnp.zeros((D, MISC_W - IDX_DIM - IDX_HEADS - GLA_GATE_RANK), w_in.dtype)
    wm = jnp.concatenate([col(4), col(5), col(10), pad], axis=1).astype(BF16)
    wg = jnp.concatenate([col(6), col(7), col(8), col(9)], axis=1).astype(BF16)
    wt = jnp.concatenate([col(0), col(3), col(2), col(8), wm], axis=1).T.astype(BF16)

    row = lambda w, dt: jax.ShapeDtypeStruct((B, L, w), dt)
    feat = lambda w, dt: jax.ShapeDtypeStruct((B, w, L), dt)
    row_spec = lambda w: pl.BlockSpec((1, tm, w), lambda b, i: (b, i, 0))
    feat_spec = lambda w: pl.BlockSpec((1, w, tm), lambda b, i: (b, 0, i))
    full = lambda a: pl.BlockSpec(a.shape, lambda b, i: (0,) * a.ndim,
                                  pipeline_mode=pl.Buffered(1))
    vec = pl.BlockSpec((1, 1, D), lambda b, i: (b, 0, 0))
    return pl.pallas_call(
        _inproj_body,
        grid=(B, L // tm),
        in_specs=[row_spec(D), vec, vec, full(norm1_g), full(wa), full(wm), full(wg), full(wt)],
        out_specs=[feat_spec(ATT_WIDTH), row_spec(ATT_WIDTH),
                   feat_spec(IDX_HEADS * IDX_DIM), row_spec(MISC_W), row_spec(IDX_DIM),
                   row_spec(GLA_KW), row_spec(GLA_KW), row_spec(GLA_WIDTH), row_spec(GLA_WIDTH),
                   feat_spec(ATT_WIDTH), feat_spec(GLA_WIDTH), feat_spec(MISC_W)],
        out_shape=[feat(ATT_WIDTH, BF16), row(ATT_WIDTH, BF16),
                   feat(IDX_HEADS * IDX_DIM, BF16), row(MISC_W, F32), row(IDX_DIM, BF16),
                   row(GLA_KW, F32), row(GLA_KW, F32), row(GLA_WIDTH, F32), row(GLA_WIDTH, F32),
                   feat(ATT_WIDTH, BF16), feat(GLA_WIDTH, F32), feat(MISC_W, F32)],
        compiler_params=pltpu.CompilerParams(
            dimension_semantics=("parallel", "parallel"), vmem_limit_bytes=VMEM_LIMIT),
    )(x, sc1, sh1, norm1_g, wa, wm, wg, wt)


def _bit_transpose(words):
    x = list(words)
    for shift, mask in ((16, 0x0000FFFF), (8, 0x00FF00FF), (4, 0x0F0F0F0F),
                        (2, 0x33333333), (1, 0x55555555)):
        for lo in range(WORD_BITS):
            if lo & shift == 0:
                a, b = x[lo], x[lo + shift]
                t = (a ^ lax.shift_right_logical(b, shift)) & mask
                x[lo] = a ^ t
                x[lo + shift] = b ^ lax.shift_left(t, shift)
    return x


def _dsa_body(qit_ref, misct_ref, ki_ref, qt_ref, k_ref, vt_ref, o_ref,
              keys_ref, planes_ref, cand_ref, dta_ref, dtb_ref, qka_ref, qkb_ref,
              mxa_ref, mxb_ref, m_ref, acc_ref, *, n_sel, seq_len):
    i = pl.program_id(1)
    tq, ck = Q_TILE, K_CHUNK
    n_full = (i * tq) // ck
    n_chunks = n_full + 1
    idx_scale = (IDX_HEADS ** -0.5) * (IDX_DIM ** -0.5)

    s_iota = lax.broadcasted_iota(I32, (ck, tq), 0)
    t_pos = i * tq + lax.broadcasted_iota(I32, (ck, tq), 1)
    w = misct_ref[0][W_IDX_OFF:W_IDX_OFF + IDX_HEADS, :] * idx_scale
    qit_heads = [qit_ref[0, h * IDX_DIM:(h + 1) * IDX_DIM, :] for h in range(IDX_HEADS)]
    w_rows = [w[h:h + 1, :] for h in range(IDX_HEADS)]

    half = GROUP_KEYS
    s_half = lax.broadcasted_iota(I32, (half, tq), 0)
    t_half = i * tq + lax.broadcasted_iota(I32, (half, tq), 1)

    def dots_to(buf, u):
        kc = ki_ref[0, pl.ds(pl.multiple_of(u * half, half), half), :]
        for h in range(IDX_HEADS):
            buf[h] = _dot(kc, qit_heads[h])

    def keys_from(buf, c, second, diagonal):
        u = 2 * c + second
        score = jnp.zeros((half, tq), F32)
        for h in range(IDX_HEADS):
            score = score + jnp.maximum(buf[h], 0.0) * w_rows[h]
        bits = lax.bitcast_convert_type(score, I32)
        key = jnp.where(bits < 0, INT_MIN - bits, bits)
        if diagonal:
            key = jnp.where(u * half + s_half <= t_half, key, KEY_MASKED)
        keys_ref[c, second * half:(second + 1) * half, :] = key

    dots_to(dta_ref, 0)

    def score_pair(c, carry):
        dots_to(dtb_ref, 2 * c + 1)
        keys_from(dta_ref, c, 0, diagonal=False)
        dots_to(dta_ref, 2 * c + 2)
        keys_from(dtb_ref, c, 1, diagonal=False)
        return carry

    lax.fori_loop(0, n_full, score_pair, 0)
    n_halves = GROUPS_PER_CHUNK * n_chunks

    @pl.when(i % 2 == 0)
    def _():
        keys_from(dta_ref, n_full, 0, diagonal=True)
        keys_ref[n_full, half:, :] = jnp.full((half, tq), KEY_MASKED, I32)

    @pl.when(i % 2 == 1)
    def _():
        dots_to(dtb_ref, i)
        keys_from(dta_ref, n_full, 0, diagonal=False)
        keys_from(dtb_ref, n_full, 1, diagonal=True)

    def slice_chunk(c, carry):
        for second in range(GROUPS_PER_CHUNK):
            u = GROUPS_PER_CHUNK * c + second
            for lb in range(tq // LANES):
                lanes = slice(lb * LANES, (lb + 1) * LANES)
                rows = [keys_ref[c, second * half + m * SUBLANES:
                                 second * half + (m + 1) * SUBLANES, lanes]
                        for m in range(WORD_BITS)]
                for p, plane in enumerate(_bit_transpose(rows)):
                    planes_ref[p, u, :, lanes] = ~plane if p == 0 else plane
            cand_ref[u] = jnp.full((SUBLANES, tq), -1, I32)
        return carry

    lax.fori_loop(0, n_chunks, slice_chunk, 0)

    sg = SEARCH_GROUPS
    n_steps = (n_halves + sg - 1) // sg

    def pad_group(g, carry):
        planes_ref[:, g] = jnp.zeros((WORD_BITS, SUBLANES, tq), I32)
        cand_ref[g] = jnp.zeros((SUBLANES, tq), I32)
        return carry

    lax.fori_loop(n_halves, sg * n_steps, pad_group, 0)

    def count_over_groups(words_of):
        def body(step, acc):
            words = words_of(pl.multiple_of(step * sg, sg))
            return acc + jnp.sum(lax.population_count(words).astype(F32), axis=0)
        acc = lax.fori_loop(0, n_steps, body, jnp.zeros((SUBLANES, tq), F32))
        return jnp.sum(acc, axis=0, keepdims=True)

    def decide(cnt, rank, kth_u):
        take = cnt >= rank
        return (jnp.where(take, 0, -1), jnp.where(take, rank, rank - cnt),
                lax.shift_left(kth_u, 1) | jnp.where(take, 1, 0))

    carry = decide(
        count_over_groups(lambda g0: cand_ref[pl.ds(g0, sg)] & planes_ref[0, pl.ds(g0, sg)]),
        jnp.full((1, tq), float(n_sel), F32), jnp.zeros((1, tq), I32))

    def search_pass(p, carry):
        flip, rank, kth_u = carry

        def candidates_with_bit(g0):
            gs = pl.ds(g0, sg)
            cand = cand_ref[gs] & (planes_ref[p - 1, gs] ^ flip)
            cand_ref[gs] = cand
            return cand & planes_ref[p, gs]

        return decide(count_over_groups(candidates_with_bit), rank, kth_u)

    flip, rank, kth_u = lax.fori_loop(1, WORD_BITS, search_pass, carry)

    def equal_to_threshold(g0):
        gs = pl.ds(g0, sg)
        cand = cand_ref[gs] & (planes_ref[WORD_BITS - 1, gs] ^ flip)
        cand_ref[gs] = cand
        return cand

    n_equal = count_over_groups(equal_to_threshold)
    kth = kth_u ^ INT_MIN
    has_tie = (n_equal > rank) & (kth > KEY_MASKED)
    kth = jnp.maximum(kth, KEY_MASKED + 1)

    @pl.when(jnp.max(jnp.where(has_tie, 1.0, 0.0)) > 0.0)
    def _():
        grp = lax.broadcasted_iota(I32, (sg, SUBLANES, tq), 0)
        sub = lax.broadcasted_iota(I32, (sg, SUBLANES, tq), 1)
        sub_bits = SUBLANES.bit_length() - 1

        def ties_before(bound):
            def words_of(g0):
                first = (g0 + grp) * GROUP_KEYS + sub
                n_low = lax.shift_right_arithmetic(bound - first + (SUBLANES - 1), sub_bits)
                n_low = jnp.clip(n_low, 0, WORD_BITS)
                mask = jnp.where(n_low > 0,
                                 lax.shift_left(jnp.int32(-1), WORD_BITS - jnp.maximum(n_low, 1)), 0)
                return cand_ref[pl.ds(g0, sg)] & mask
            return count_over_groups(words_of)

        n_bits = (seq_len - 1).bit_length()

        def idx_step(it, lo):
            cand = lo + lax.shift_left(jnp.int32(1), n_bits - 1 - it)
            return jnp.where(ties_before(cand) < rank, cand, lo)

        last = lax.fori_loop(0, n_bits, idx_step, jnp.zeros((1, tq), I32))
        last_kept = jnp.where(has_tie, last, seq_len)

        def demote(c, carry):
            kk = keys_ref[c]
            s_pos = c * ck + s_iota
            drop = (kk == kth) & (s_pos > last_kept)
            keys_ref[c] = jnp.where(drop, kk - 1, kk)
            return carry

        lax.fori_loop(0, n_chunks, demote, 0)

    hd = ATT_HEAD_DIM
    qt_heads = [qt_ref[0, h * hd:(h + 1) * hd, :] for h in range(ATT_HEADS)]
    ones_rows = jnp.ones((PACKED_ROWS, ck), BF16)

    heads = [slice(h * hd, (h + 1) * hd) for h in range(ATT_HEADS)]

    def qk_to(slot, c):
        buf, cmax = slot
        kc = k_ref[0, pl.ds(pl.multiple_of(c * ck, ck), ck), :]
        bias = jnp.where(keys_ref[c] >= kth, 0.0, MASKED_LOGIT)
        for h, sl in enumerate(heads):
            logits = _dot(kc[:, sl], qt_heads[h]) + bias
            buf[h] = logits
            cmax[h] = jnp.max(logits, axis=0, keepdims=True)

    def softmax_pv(c, slot):
        buf, cmax = slot
        start = pl.multiple_of(c * ck, ck)
        ms = [m_ref[h] for h in range(ATT_HEADS)]
        new_ms = [jnp.maximum(ms[h], cmax[h]) for h in range(ATT_HEADS)]
        ps = [jnp.exp2(buf[h] - new_ms[h]).astype(BF16) for h in range(ATT_HEADS)]
        for h, sl in enumerate(heads):
            vt_ext = jnp.concatenate([vt_ref[0, sl, pl.ds(start, ck)], ones_rows], axis=0)
            acc_ref[h] = jnp.exp2(ms[h] - new_ms[h]) * acc_ref[h] + _dot(vt_ext, ps[h])
            m_ref[h] = new_ms[h]

    slot_a, slot_b = (qka_ref, mxa_ref), (qkb_ref, mxb_ref)
    m_ref[...] = jnp.full_like(m_ref, MASKED_LOGIT)
    acc_ref[...] = jnp.zeros_like(acc_ref)
    qk_to(slot_a, 0)

    def attend_pair(j, carry):
        c = 2 * j
        qk_to(slot_b, c + 1)
        softmax_pv(c, slot_a)
        qk_to(slot_a, c + 2)
        softmax_pv(c + 1, slot_b)
        return carry

    lax.fori_loop(0, n_full // 2, attend_pair, 0)

    @pl.when(n_full % 2 == 0)
    def _():
        softmax_pv(n_full, slot_a)

    @pl.when(n_full % 2 == 1)
    def _():
        qk_to(slot_b, n_full)
        softmax_pv(n_full - 1, slot_a)
        softmax_pv(n_full, slot_b)

    for h in range(ATT_HEADS):
        o_ref[0, h * hd:(h + 1) * hd, :] = acc_ref[h, :hd, :] / acc_ref[h, hd:hd + 1, :]


def _dsa_attention(qt, k, vt, qit, ki, misct):
    B, L, _ = k.shape
    tq = Q_TILE
    assert tq == GROUP_KEYS and K_CHUNK == 2 * tq and L % K_CHUNK == 0
    n_sel = min(TOPK_MAX, L // 4)
    assert n_sel <= K_CHUNK
    body = functools.partial(_dsa_body, n_sel=n_sel, seq_len=L)
    feat_spec = lambda w: pl.BlockSpec((1, w, tq), lambda b, i: (b, 0, i))
    seq_spec = lambda a: pl.BlockSpec((1,) + a.shape[1:], lambda b, i: (b, 0, 0),
                                      pipeline_mode=pl.Buffered(1))
    n_groups = pl.cdiv(GROUPS_PER_CHUNK * (L // K_CHUNK), SEARCH_GROUPS) * SEARCH_GROUPS
    return pl.pallas_call(
        body,
        grid=(B, L // tq),
        in_specs=[feat_spec(IDX_HEADS * IDX_DIM), feat_spec(MISC_W), seq_spec(ki),
                  feat_spec(ATT_WIDTH), seq_spec(k), seq_spec(vt)],
        out_specs=feat_spec(ATT_WIDTH),
        out_shape=jax.ShapeDtypeStruct((B, ATT_WIDTH, L), F32),
        scratch_shapes=[
            pltpu.VMEM((L // K_CHUNK, K_CHUNK, tq), I32),
            pltpu.VMEM((WORD_BITS, n_groups, SUBLANES, tq), I32),
            pltpu.VMEM((n_groups, SUBLANES, tq), I32),
            pltpu.VMEM((IDX_HEADS, GROUP_KEYS, tq), F32), pltpu.VMEM((IDX_HEADS, GROUP_KEYS, tq), F32),
            pltpu.VMEM((ATT_HEADS, K_CHUNK, tq), F32), pltpu.VMEM((ATT_HEADS, K_CHUNK, tq), F32),
            pltpu.VMEM((ATT_HEADS, 1, tq), F32), pltpu.VMEM((ATT_HEADS, 1, tq), F32),
            pltpu.VMEM((ATT_HEADS, 1, tq), F32),
            pltpu.VMEM((ATT_HEADS, ATT_HEAD_DIM + PACKED_ROWS, tq), F32)],
        compiler_params=pltpu.CompilerParams(
            dimension_semantics=("parallel", "arbitrary"), vmem_limit_bytes=VMEM_LIMIT),
    )(qit, misct, ki, qt, k, vt)


def _gla_body(gq_ref, gk_ref, gvt_ref, gv_ref, misc_ref, w2_ref, b2_ref, o_ref, st_ref, b_ref):
    n, kw, vw = GLA_SUB, GLA_KW, GLA_WIDTH

    @pl.when(pl.program_id(1) == 0)
    def _():
        st_ref[...] = jnp.zeros_like(st_ref)

    rows = gq_ref.shape[1]
    grp = GLA_GROUP
    lane_head = lax.broadcasted_iota(I32, (n, kw), 1) // GLA_DK
    ed = lax.broadcasted_iota(I32, (kw, vw), 0) // GLA_DK
    ev = lax.broadcasted_iota(I32, (kw, vw), 1) // GLA_DV
    expand = jnp.where(ed == ev, 1.0, 0.0).astype(BF16)
    row_i = lax.broadcasted_iota(I32, (n, kw), 0)
    in_blk = lax.broadcasted_iota(I32, (rows, kw), 0) % n
    scale = GLA_DK ** -0.5

    g_lr = misc_ref[0][:, G_LR_OFF:G_LR_OFF + GLA_GATE_RANK]
    z = jnp.dot(g_lr, w2_ref[...], precision=HIGHEST, preferred_element_type=F32) + b2_ref[...]
    log_a = (jnp.minimum(z, 0.0) - jnp.log1p(jnp.exp(-jnp.abs(z)))) * (1.0 / GLA_TAU)
    b_tile = log_a * LOG2_E
    shift = 1
    while shift < n:
        b_tile = b_tile + jnp.where(in_blk >= shift, pltpu.roll(b_tile, shift, axis=0), 0.0)
        shift *= 2
    b_ref[...] = b_tile

    def group(gi, carry):
        g0 = pl.multiple_of(gi * grp, grp)
        b_all = b_ref[pl.ds(g0, grp), :]
        q_all = gq_ref[0, pl.ds(g0, grp), :] * scale
        k_all = gk_ref[0, pl.ds(g0, grp), :]
        v_all = gv_ref[0, pl.ds(g0, grp), :]
        vt_all = gvt_ref[0, :, pl.ds(g0, grp)]
        st = st_ref[...]
        for blk in range(grp // n):
            sl = slice(blk * n, (blk + 1) * n)
            st = _gla_block(q_all[sl], k_all[sl], v_all[sl], vt_all[:, sl], b_all[sl],
                            o_ref.at[0, pl.ds(g0 + blk * n, n), :], st,
                            lane_head, expand, row_i)
        st_ref[...] = st
        return carry

    lax.fori_loop(0, rows // grp, group, 0)


def _gla_block(qb, kb, vb, vtb, b, o_blk, st, lane_head, expand, row_i):
    n, vw = GLA_SUB, GLA_WIDTH
    o_inter = _dot_nt((qb * jnp.exp2(b)).astype(BF16), st.astype(BF16))
    terms = []
    for j in range(n):
        diff = jnp.where(row_i >= j, b - b[j:j + 1, :], -jnp.inf)
        terms.append(qb * kb[j:j + 1, :] * jnp.exp2(diff))
    t_all = jnp.concatenate(terms, axis=0).astype(BF16)
    g_all = _dot(t_all, expand)
    o_intra = jnp.zeros((n, vw), F32)
    for j in range(n):
        o_intra = o_intra + g_all[j * n:(j + 1) * n, :] * vb[j:j + 1, :]
    o_blk[...] = o_inter + o_intra
    b_last = b[n - 1:n, :]
    k_dec = kb * jnp.exp2(b_last - b)
    vtb16 = vtb.astype(BF16)
    upd = jnp.concatenate(
        [_dot(vtb16[h * GLA_DV:(h + 1) * GLA_DV, :],
              jnp.where(lane_head == h, k_dec, 0.0).astype(BF16)) for h in range(GLA_HEADS)],
        axis=0)
    return st * jnp.exp2(b_last) + upd


def _gla(gq, gk, gvt, gv, misc, w_gate2, b_gate2):
    B, L, _ = gq.shape
    rows = min(GLA_ROWS, L)
    row_spec = lambda w: pl.BlockSpec((1, rows, w), lambda b, i: (b, i, 0))
    full = lambda a: pl.BlockSpec(a.shape, lambda b, i: (0,) * a.ndim)
    return pl.pallas_call(
        _gla_body,
        grid=(B, L // rows),
        in_specs=[row_spec(GLA_KW), row_spec(GLA_KW),
                  pl.BlockSpec((1, GLA_WIDTH, rows), lambda b, i: (b, 0, i)),
                  row_spec(GLA_WIDTH), row_spec(MISC_W), full(w_gate2), full(b_gate2)],
        out_specs=row_spec(GLA_WIDTH),
        out_shape=jax.ShapeDtypeStruct((B, L, GLA_WIDTH), F32),
        scratch_shapes=[pltpu.VMEM((GLA_WIDTH, GLA_KW), F32), pltpu.VMEM((rows, GLA_KW), F32)],
        compiler_params=pltpu.CompilerParams(
            dimension_semantics=("parallel", "arbitrary"), vmem_limit_bytes=VMEM_LIMIT),
    )(gq, gk, gvt, gv, misc, w_gate2, b_gate2)


def _rms(x, g):
    ms = jnp.mean(x * x, axis=-1, keepdims=True)
    return (x * lax.rsqrt(ms + NORM_EPS)) * g


def _mix_ffn_body(x_ref, attt_ref, gla_ref, go_ref, g1_ref, sc_ref, sh_ref, g2_ref,
                  ag_ref, gg_ref, ng_ref, fg_ref, wo_ref, wgu_ref, wd_ref, o_ref, *, d_ff, final):
    rows = x_ref.shape[1]
    halves = [slice(r * rows // 2, (r + 1) * rows // 2) for r in range(2)]

    def mixed(rs):
        a = _rms(attt_ref[0, :, rs].T, ag_ref[...])
        gl = gla_ref[0, rs, :]
        parts = []
        for h in range(GLA_HEADS):
            sl = slice(h * GLA_DV, (h + 1) * GLA_DV)
            parts.append(_rms(gl[:, sl], gg_ref[...]))
        g = jnp.concatenate(parts, axis=1) * _silu(go_ref[0, rs, :])
        mix = (_dot(a.astype(BF16), wo_ref[0:ATT_WIDTH, :])
               + _dot(g.astype(BF16), wo_ref[ATT_WIDTH:, :]))
        return x_ref[0, rs, :] + g1_ref[0] * mix

    xs = [mixed(rs) for rs in halves]
    hs = [(_rms(x, ng_ref[...]) * (1.0 + sc_ref[0]) + sh_ref[0]).astype(BF16) for x in xs]
    gus = [_dot(h, wgu_ref[...]) for h in hs]
    acts = [(_silu(gu[:, :d_ff]) * gu[:, d_ff:]).astype(BF16) for gu in gus]
    for rs, x, act in zip(halves, xs, acts):
        y = x + g2_ref[0] * _dot(act, wd_ref[...])
        o_ref[0, rs, :] = _rms(y, fg_ref[...]) if final else y


def _mix_ffn(x, att_t, gla, go, g1, sc2, sh2, g2, att_out_g, gla_out_g, norm2_g, final_g,
             w_out, w_gate_up, w_down, final):
    B, L, D = x.shape
    d_ff = w_down.shape[0]
    tm = min(ROWS_FFN, L)
    row_spec = lambda w: pl.BlockSpec((1, tm, w), lambda b, i: (b, i, 0))
    full = lambda a: pl.BlockSpec(a.shape, lambda b, i: (0,) * a.ndim,
                                  pipeline_mode=pl.Buffered(1))
    vec = pl.BlockSpec((1, 1, D), lambda b, i: (b, 0, 0))
    wo = w_out.astype(BF16)
    wgu = w_gate_up.astype(BF16)
    wd = w_down.astype(BF16)
    return pl.pallas_call(
        functools.partial(_mix_ffn_body, d_ff=d_ff, final=final),
        grid=(B, L // tm),
        in_specs=[row_spec(D), pl.BlockSpec((1, ATT_WIDTH, tm), lambda b, i: (b, 0, i)),
                  row_spec(GLA_WIDTH), row_spec(GLA_WIDTH), vec, vec, vec, vec,
                  full(att_out_g), full(gla_out_g), full(norm2_g), full(final_g),
                  full(wo), full(wgu), full(wd)],
        out_specs=row_spec(D),
        out_shape=jax.ShapeDtypeStruct((B, L, D), F32),
        compiler_params=pltpu.CompilerParams(
            dimension_semantics=("parallel", "parallel"), vmem_limit_bytes=VMEM_LIMIT),
    )(x, att_t, gla, go, g1, sc2, sh2, g2, att_out_g, gla_out_g, norm2_g, final_g, wo, wgu, wd)


def kernel(x, c, w_mod, b_mod, norm1_g, w_in, w_gate2, b_gate2, att_out_g, gla_out_g,
           w_out, norm2_g, w_gate_up, w_down, final_g):
    B, L, D = x.shape
    depth = w_mod.shape[0]
    for l in range(depth):
        mod = _modulation(c, w_mod[l], b_mod[l][None, :])
        sh1, sc1, g1, sh2, sc2, g2 = [m[:, None, :] for m in jnp.split(mod, N_MOD, axis=-1)]
        qt, k, qit, misc, ki, gq, gk, gv, go, vt, gvt, misct = _input_projection(
            x, sc1, sh1, norm1_g[l][None, :], w_in[l])
        att_t = _dsa_attention(qt, k, vt, qit, ki, misct)
        gla = _gla(gq, gk, gvt, gv, misc, w_gate2[l], b_gate2[l][None, :])
        x = _mix_ffn(x, att_t, gla, go, g1, sc2, sh2, g2, att_out_g[l][None, :],
                     gla_out_g[l][None, :], norm2_g[l][None, :], final_g[None, :],
                     w_out[l], w_gate_up[l], w_down[l], final=(l == depth - 1))
    return x
```
